```python
import math
import jax, jax.numpy as jnp
from jax import lax
import numpy as np

D_MODEL = 1024
BATCH = 4
SEQ = 4096
DEPTH = 4
DEC_BATCH = 32
DEC_SEQ = 4
PAST_LEN = 8192
PAGE_SIZE = 128

N_LIN_LAYERS = (DEPTH + 1) // 2
N_SB_LAYERS = DEPTH // 2
DV = 128
H_GLA = D_MODEL // (2 * DV)
DK_GLA = DV // 2
GLA_RANK = 16
GLA_TAU = 16.0
H_GDN = D_MODEL // (2 * DV)
DK_GDN = 128
CONV_W = 4
CHUNK = 64
H_SB = D_MODEL // 64
DH_SB = 64
SB_BLOCK = 128
SB_BIAS_INIT = -8.0
N_EXPERTS = 64
TOP_K = 8
N_GROUPS = 8
TOPK_GROUPS = 4
D_EXPERT = 256
D_SHARED = 256
ROUTED_SCALE = 2.5
MOE_BLOCK = 1024
DEEPNORM_ALPHA = (2.0 * DEPTH) ** 0.25
DEEPNORM_BETA = (8.0 * DEPTH) ** -0.25
LN_EPS = 1e-5
RMS_EPS = 1e-6

GLA_QK = H_GLA * DK_GLA
GLA_VW = H_GLA * DV
GDN_QK = H_GDN * DK_GDN
GDN_VW = H_GDN * DV
GDN_CONV_CH = 2 * GDN_QK + GDN_VW
LIN_SPLITS = (GLA_QK, GLA_QK, GLA_VW, GLA_VW, GLA_RANK, GDN_QK, GDN_QK, GDN_VW, GDN_VW, H_GDN, H_GDN)
LIN_SPLIT_IDX = tuple(int(i) for i in np.cumsum(LIN_SPLITS)[:-1])
P_LIN = sum(LIN_SPLITS)
MIX_W = GLA_VW + GDN_VW
SB_W = H_SB * DH_SB

kernel_name = 'hybrid_gla_gdn_stickbreak_moe_step'


def layer_norm(x, g, b):
    xf = x.astype(jnp.float32)
    mu = jnp.mean(xf, -1, keepdims=True)
    var = jnp.mean(jnp.square(xf - mu), -1, keepdims=True)
    return ((xf - mu) * lax.rsqrt(var + LN_EPS) * g + b).astype(x.dtype)


def rms_norm(x, g):
    xf = x.astype(jnp.float32)
    return (xf * lax.rsqrt(jnp.mean(xf * xf, -1, keepdims=True) + RMS_EPS) * g).astype(x.dtype)


def l2_norm(x):
    xf = x.astype(jnp.float32)
    return xf * lax.rsqrt(jnp.sum(xf * xf, -1, keepdims=True) + RMS_EPS)


def to_chunks(x, c):
    b, l, h, d = x.shape
    return x.reshape(b, l // c, c, h, d).transpose(1, 0, 3, 2, 4)


def from_chunks(x):
    nc, b, h, c, d = x.shape
    return x.transpose(1, 0, 3, 2, 4).reshape(b, nc * c, h, d)


def gla_chunked(q, k, v, log_a, s0):
    f32 = jnp.float32
    c = math.gcd(q.shape[1], CHUNK)
    qc, kc, vc, gc = (to_chunks(t.astype(f32), c) for t in (q, k, v, log_a))
    causal = jnp.tril(jnp.ones((c, c), bool))

    def step(s, inp):
        qi, ki, vi, gi = inp
        bcum = jnp.cumsum(gi, axis=2)
        blast = bcum[:, :, -1:, :]
        diff = bcum[:, :, :, None, :] - bcum[:, :, None, :, :]
        decay = jnp.exp(jnp.where(causal[:, :, None], diff, -jnp.inf))
        att = jnp.einsum('bhid,bhjd,bhijd->bhij', qi, ki, decay)
        o = (jnp.einsum('bhcd,bhdv->bhcv', qi * jnp.exp(bcum), s)
             + jnp.einsum('bhij,bhjv->bhiv', att, vi))
        s = (s * jnp.exp(blast)[:, :, 0, :, None]
             + jnp.einsum('bhcd,bhcv->bhdv', ki * jnp.exp(blast - bcum), vi))
        return s, o

    s, o = lax.scan(step, s0.astype(f32), (qc, kc, vc, gc))
    return from_chunks(o), s


def gdn_chunked(q, k, v, g, beta, s0):
    f32 = jnp.float32
    c = math.gcd(q.shape[1], CHUNK)
    qc, kc, vc = (to_chunks(t.astype(f32), c) for t in (q, k, v))
    gc = to_chunks(g[..., None].astype(f32), c)[..., 0]
    bc = to_chunks(beta[..., None].astype(f32), c)[..., 0]
    gam = jnp.cumsum(gc, -1)
    incl = jnp.tril(jnp.ones((c, c), bool))
    strict = jnp.tril(jnp.ones((c, c), bool), -1)
    dec = jnp.exp(jnp.where(incl, gam[..., :, None] - gam[..., None, :], -jnp.inf))
    kk = jnp.einsum('nbhid,nbhjd->nbhij', kc, kc)
    lower = jnp.where(strict, kk * dec * bc[..., :, None], 0.0) + jnp.eye(c, dtype=f32)
    rhs = jnp.concatenate([vc * bc[..., None], kc * (bc * jnp.exp(gam))[..., None]], -1)
    sol = lax.linalg.triangular_solve(lower, rhs, left_side=True, lower=True, unit_diagonal=True)
    u0, w = sol[..., :DV], sol[..., DV:]
    qk = jnp.einsum('nbhid,nbhjd->nbhij', qc, kc) * dec
    q_dec = qc * jnp.exp(gam)[..., None]
    k_dec = kc * jnp.exp(gam[..., -1:] - gam)[..., None]
    g_last = jnp.exp(gam[..., -1])

    def step(s, inp):
        u0i, wi, qki, qdi, kdi, gli = inp
        u = u0i - jnp.einsum('bhcd,bhdv->bhcv', wi, s)
        o = jnp.einsum('bhcd,bhdv->bhcv', qdi, s) + jnp.einsum('bhij,bhjv->bhiv', qki, u)
        s = s * gli[..., None, None] + jnp.einsum('bhcd,bhcv->bhdv', kdi, u)
        return s, o

    s, o = lax.scan(step, s0.astype(f32), (u0, w, qk, q_dec, k_dec, g_last))
    return from_chunks(o), s


def causal_conv(x, prev, w):
    l = x.shape[1]
    xp = jnp.concatenate([prev.astype(x.dtype), x], 1)
    y = xp[:, 0:l] * w[0]
    for i in range(1, CONV_W):
        y = y + xp[:, i:i + l] * w[i]
    return y, xp[:, xp.shape[1] - (CONV_W - 1):]


def lin_mixer(h, gla_s0, gdn_s0, conv_prev, w_in, wa2, ba, gla_g, conv_w, a_log, dt_bias, gdn_g, w_out):
    b, l, _ = h.shape
    f32 = jnp.float32
    gq, gk, gv, gr, ga, dq, dk, dv, dz, da, db = jnp.split(h @ w_in, LIN_SPLIT_IDX, axis=-1)
    q = gq.reshape(b, l, H_GLA, DK_GLA) * DK_GLA ** -0.5
    k = gk.reshape(b, l, H_GLA, DK_GLA)
    v = gv.reshape(b, l, H_GLA, DV)
    log_a = jax.nn.log_sigmoid((ga @ wa2 + ba).astype(f32)) / GLA_TAU
    o_gla, gla_s = gla_chunked(q, k, v, log_a.reshape(b, l, H_GLA, DK_GLA), gla_s0)
    o_gla = rms_norm(o_gla, gla_g) * jax.nn.silu(gr.reshape(b, l, H_GLA, DV))
    qkv, conv_new = causal_conv(jnp.concatenate([dq, dk, dv], -1), conv_prev, conv_w)
    cq, ck, cv = jnp.split(jax.nn.silu(qkv), (GDN_QK, 2 * GDN_QK), axis=-1)
    q = l2_norm(cq.reshape(b, l, H_GDN, DK_GDN)) * DK_GDN ** -0.5
    k = l2_norm(ck.reshape(b, l, H_GDN, DK_GDN))
    v = cv.reshape(b, l, H_GDN, DV)
    beta = jax.nn.sigmoid(db.astype(f32))
    g = -jnp.exp(a_log.astype(f32)) * jax.nn.softplus(da.astype(f32) + dt_bias)
    o_gdn, gdn_s = gdn_chunked(q, k, v, g, beta, gdn_s0)
    o_gdn = rms_norm(o_gdn, gdn_g) * jax.nn.silu(dz.reshape(b, l, H_GDN, DV))
    o = jnp.concatenate([o_gla.reshape(b, l, GLA_VW), o_gdn.reshape(b, l, GDN_VW)], -1).astype(h.dtype)
    return o @ w_out, gla_s, gdn_s, conv_new


def sb_block(q, k, v, q_pos, k_pos, bias):
    z = (jnp.einsum('bhqd,bhsd->bhqs', q, k).astype(jnp.float32) * DH_SB ** -0.5
         + bias.astype(jnp.float32)[:, None, None])
    mask = k_pos[None, :] < q_pos[:, None]
    sp = jnp.where(mask, jax.nn.softplus(z), 0.0)
    rcs = lax.cumsum(sp, axis=3, reverse=True)
    a = jnp.exp(jnp.where(mask, z - rcs, -jnp.inf))
    return jnp.einsum('bhqs,bhsd->bhqd', a.astype(v.dtype), v)


def sb_attend(q, k, v, q_pos, k_pos, bias):
    b, lq, h, d = q.shape
    qb = math.gcd(lq, SB_BLOCK)
    kt = k.transpose(0, 2, 1, 3)
    vt = v.transpose(0, 2, 1, 3)
    qt = q.reshape(b, lq // qb, qb, h, d).transpose(1, 0, 3, 2, 4)
    pos = q_pos.reshape(lq // qb, qb)
    o = lax.map(lambda a: sb_block(a[0], kt, vt, a[1], k_pos, bias), (qt, pos))
    return o.transpose(1, 0, 3, 2, 4).reshape(b, lq, h, d)


def sb_mixer(h, k_past, v_past, w_qkv, w_o, bias):
    b, l, _ = h.shape
    q, k, v = (t.reshape(b, l, H_SB, DH_SB) for t in jnp.split(h @ w_qkv, 3, axis=-1))
    p = k_past.shape[1]
    k_all = jnp.concatenate([k_past.astype(k.dtype), k], 1)
    v_all = jnp.concatenate([v_past.astype(v.dtype), v], 1)
    o = sb_attend(q, k_all, v_all, p + jnp.arange(l), jnp.arange(p + l), bias)
    return o.reshape(b, l, SB_W) @ w_o, k, v


def moe(h, w_r, e_b, w_g, w_u, w_d, ws_g, ws_u, ws_d):
    b, l, d = h.shape
    t = h.reshape(b * l, d)
    n = t.shape[0]
    s = jax.nn.sigmoid((t @ w_r).astype(jnp.float32))
    sel = s + e_b
    grp = sel.reshape(n, N_GROUPS, N_EXPERTS // N_GROUPS)
    gscore = jnp.sum(lax.top_k(grp, 2)[0], -1)
    _, gidx = lax.top_k(gscore, TOPK_GROUPS)
    gmask = jnp.sum(jax.nn.one_hot(gidx, N_GROUPS), -2) > 0
    emask = jnp.repeat(gmask, N_EXPERTS // N_GROUPS, axis=-1)
    _, idx = lax.top_k(jnp.where(emask, sel, -jnp.inf), TOP_K)
    wk = jnp.take_along_axis(s, idx, -1)
    wk = wk / jnp.sum(wk, -1, keepdims=True) * ROUTED_SCALE
    gates = jnp.einsum('tk,tke->te', wk, jax.nn.one_hot(idx, N_EXPERTS))
    tb = min(MOE_BLOCK, n)
    pad = (-n) % tb
    tp = jnp.pad(t, ((0, pad), (0, 0))).reshape(-1, tb, d)
    gp = jnp.pad(gates, ((0, pad), (0, 0))).reshape(-1, tb, N_EXPERTS)

    def block(args):
        xb, gb = args
        a = jax.nn.silu(jnp.einsum('td,edf->tef', xb, w_g)) * jnp.einsum('td,edf->tef', xb, w_u)
        return jnp.einsum('tef,efd->td', a * gb[..., None].astype(a.dtype), w_d)

    y = lax.map(block, (tp, gp)).reshape(-1, d)[:n]
    shared = (jax.nn.silu(t @ ws_g) * (t @ ws_u)) @ ws_d
    return (y.astype(t.dtype) + shared).reshape(b, l, d)


def ada_mod(c, w, b):
    m = jax.nn.silu(c) @ w + b
    return jnp.split(m[:, None, :], 6, axis=-1)


def setup_inputs(seed: int = 0) -> dict:
    key = jax.random.key(seed)
    keys = list(jax.random.split(key, 40))
    cnt = [0]

    def nk():
        cnt[0] += 1
        return keys[cnt[0] - 1]

    def nrm(shape, scale):
        return jax.random.normal(nk(), shape, jnp.float32) * scale

    n_pages = PAST_LEN // PAGE_SIZE
    n_used = DEC_BATCH * n_pages
    n_phys = n_used + max(1, n_used // 4)
    lin_col = jnp.asarray(np.concatenate([np.full(n, DEEPNORM_BETA if i in (2, 7) else 1.0, np.float32)
                                          for i, n in enumerate(LIN_SPLITS)]))
    sb_col = jnp.asarray(np.concatenate([np.ones(2 * SB_W, np.float32), np.full(SB_W, DEEPNORM_BETA, np.float32)]))
    perm = jax.random.permutation(nk(), n_phys)[:n_used]
    page_table = perm.reshape(DEC_BATCH, n_pages).astype(jnp.int32)
    a_log = jnp.log(jax.random.uniform(nk(), (N_LIN_LAYERS, H_GDN), jnp.float32, 1.0, 16.0))
    dt = jnp.exp(jax.random.uniform(nk(), (N_LIN_LAYERS, H_GDN), jnp.float32, math.log(1e-3), math.log(1e-1)))
    dt_bias = dt + jnp.log(-jnp.expm1(-dt))
    return {
        'x_prompt': nrm((BATCH, SEQ, D_MODEL), 1.0),
        'x_sample': nrm((DEC_BATCH, DEC_SEQ, D_MODEL), 1.0),
        'state_gla': nrm((N_LIN_LAYERS, DEC_BATCH, H_GLA, DK_GLA, DV), DK_GLA ** -0.5),
        'state_gdn': nrm((N_LIN_LAYERS, DEC_BATCH, H_GDN, DK_GDN, DV), DK_GDN ** -0.5),
        'state_conv': nrm((N_LIN_LAYERS, DEC_BATCH, CONV_W - 1, GDN_CONV_CH), 1.0),
        'cache_k': nrm((N_SB_LAYERS, n_phys, PAGE_SIZE, H_SB, DH_SB), 1.0),
        'cache_v': nrm((N_SB_LAYERS, n_phys, PAGE_SIZE, H_SB, DH_SB), DEEPNORM_BETA),
        'page_table': page_table,
        'c_prompt': nrm((BATCH, D_MODEL), 1.0),
        'c_sample': nrm((DEC_BATCH, D_MODEL), 1.0),
        'w_ada': nrm((DEPTH, D_MODEL, 6 * D_MODEL), 0.1 * D_MODEL ** -0.5),
        'b_ada': nrm((DEPTH, 6 * D_MODEL), 0.01),
        'ln_g': 1.0 + nrm((DEPTH, 2, D_MODEL), 0.01),
        'ln_b': nrm((DEPTH, 2, D_MODEL), 0.01),
        'w_in_lin': nrm((N_LIN_LAYERS, D_MODEL, P_LIN), D_MODEL ** -0.5) * lin_col,
        'gla_wa2': nrm((N_LIN_LAYERS, GLA_RANK, GLA_QK), GLA_RANK ** -0.5),
        'gla_ba': nrm((N_LIN_LAYERS, GLA_QK), 0.1),
        'gla_norm': 1.0 + nrm((N_LIN_LAYERS, DV), 0.01),
        'gdn_conv': nrm((N_LIN_LAYERS, CONV_W, GDN_CONV_CH), CONV_W ** -0.5),
        'gdn_a_log': a_log,
        'gdn_dt_bias': dt_bias,
        'gdn_norm': 1.0 + nrm((N_LIN_LAYERS, DV), 0.01),
        'w_out_lin': nrm((N_LIN_LAYERS, MIX_W, D_MODEL), MIX_W ** -0.5 * DEEPNORM_BETA),
        'w_qkv_sb': nrm((N_SB_LAYERS, D_MODEL, 3 * SB_W), D_MODEL ** -0.5) * sb_col,
        'w_o_sb': nrm((N_SB_LAYERS, SB_W, D_MODEL), SB_W ** -0.5 * DEEPNORM_BETA),
        'sb_bias': SB_BIAS_INIT + nrm((N_SB_LAYERS, H_SB), 0.5),
        'w_router': nrm((DEPTH, D_MODEL, N_EXPERTS), D_MODEL ** -0.5),
        'e_bias': nrm((DEPTH, N_EXPERTS), 0.01),
        'w_gate': nrm((DEPTH, N_EXPERTS, D_MODEL, D_EXPERT), D_MODEL ** -0.5),
        'w_up': nrm((DEPTH, N_EXPERTS, D_MODEL, D_EXPERT), D_MODEL ** -0.5),
        'w_down': nrm((DEPTH, N_EXPERTS, D_EXPERT, D_MODEL), D_EXPERT ** -0.5 * DEEPNORM_BETA),
        'ws_gate': nrm((DEPTH, D_MODEL, D_SHARED), D_MODEL ** -0.5),
        'ws_up': nrm((DEPTH, D_MODEL, D_SHARED), D_MODEL ** -0.5),
        'ws_down': nrm((DEPTH, D_SHARED, D_MODEL), D_SHARED ** -0.5 * DEEPNORM_BETA),
    }


def reference(x_prompt, x_sample, state_gla, state_gdn, state_conv, cache_k, cache_v, page_table,
              c_prompt, c_sample, w_ada, b_ada, ln_g, ln_b, w_in_lin, gla_wa2, gla_ba, gla_norm,
              gdn_conv, gdn_a_log, gdn_dt_bias, gdn_norm, w_out_lin, w_qkv_sb, w_o_sb, sb_bias,
              w_router, e_bias, w_gate, w_up, w_down, ws_gate, ws_up, ws_down):
    f32 = jnp.float32
    bp = x_prompt.shape[0]
    bs = x_sample.shape[0]
    xp, xs = x_prompt, x_sample
    gla_p, gla_s, gdn_p, gdn_s, conv_p, conv_s = [], [], [], [], [], []
    k_p, v_p, k_s, v_s = [], [], [], []
    for layer in range(DEPTH):
        i = layer // 2
        sh_p, sc_p, ga_p, shf_p, scf_p, gaf_p = ada_mod(c_prompt, w_ada[layer], b_ada[layer])
        sh_s, sc_s, ga_s, shf_s, scf_s, gaf_s = ada_mod(c_sample, w_ada[layer], b_ada[layer])
        hp = xp * (1.0 + sc_p) + sh_p
        hs = xs * (1.0 + sc_s) + sh_s
        if layer % 2 == 0:
            lw = (w_in_lin[i], gla_wa2[i], gla_ba[i], gla_norm[i], gdn_conv[i], gdn_a_log[i],
                  gdn_dt_bias[i], gdn_norm[i], w_out_lin[i])
            op, a1, a2, a3 = lin_mixer(hp, jnp.zeros((bp, H_GLA, DK_GLA, DV), f32),
                                       jnp.zeros((bp, H_GDN, DK_GDN, DV), f32),
                                       jnp.zeros((bp, CONV_W - 1, GDN_CONV_CH), hp.dtype), *lw)
            os_, b1, b2, b3 = lin_mixer(hs, state_gla[i], state_gdn[i], state_conv[i], *lw)
            gla_p.append(a1)
            gdn_p.append(a2)
            conv_p.append(a3)
            gla_s.append(b1)
            gdn_s.append(b2)
            conv_s.append(b3)
        else:
            empty = jnp.zeros((bp, 0, H_SB, DH_SB), hp.dtype)
            op, kk, vv = sb_mixer(hp, empty, empty, w_qkv_sb[i], w_o_sb[i], sb_bias[i])
            k_past = cache_k[i][page_table].reshape(bs, -1, H_SB, DH_SB)
            v_past = cache_v[i][page_table].reshape(bs, -1, H_SB, DH_SB)
            os_, kn, vn = sb_mixer(hs, k_past, v_past, w_qkv_sb[i], w_o_sb[i], sb_bias[i])
            k_p.append(kk)
            v_p.append(vv)
            k_s.append(kn)
            v_s.append(vn)
        xp = layer_norm(DEEPNORM_ALPHA * xp + (1.0 + ga_p) * op, ln_g[layer, 0], ln_b[layer, 0])
        xs = layer_norm(DEEPNORM_ALPHA * xs + (1.0 + ga_s) * os_, ln_g[layer, 0], ln_b[layer, 0])
        mw = (w_router[layer], e_bias[layer], w_gate[layer], w_up[layer], w_down[layer],
              ws_gate[layer], ws_up[layer], ws_down[layer])
        fp = moe(xp * (1.0 + scf_p) + shf_p, *mw)
        fs = moe(xs * (1.0 + scf_s) + shf_s, *mw)
        xp = layer_norm(DEEPNORM_ALPHA * xp + (1.0 + gaf_p) * fp, ln_g[layer, 1], ln_b[layer, 1])
        xs = layer_norm(DEEPNORM_ALPHA * xs + (1.0 + gaf_s) * fs, ln_g[layer, 1], ln_b[layer, 1])
    return (xp, xs, jnp.stack(gla_p), jnp.stack(gla_s), jnp.stack(gdn_p), jnp.stack(gdn_s),
            jnp.stack(conv_p), jnp.stack(conv_s), jnp.stack(k_p), jnp.stack(v_p),
            jnp.stack(k_s), jnp.stack(v_s))
```

```python
import functools
import math

import jax
import jax.numpy as jnp
from jax import lax
from jax.experimental import pallas as pl
from jax.experimental.pallas import tpu as pltpu

F32 = jnp.float32
BF16 = jnp.bfloat16

D_MODEL = 1024
DEPTH = 4
PAGE_SIZE = 128
DV = 128
H_GLA = 4
DK_GLA = 64
GLA_RANK = 16
GLA_TAU = 16.0
H_GDN = 4
DK_GDN = 128
CONV_W = 4
H_SB = 16
DH_SB = 64
N_EXPERTS = 64
TOP_K = 8
N_GROUPS = 8
TOPK_GROUPS = 4
D_EXPERT = 256
ROUTED_SCALE = 2.5
DEEPNORM_ALPHA = (2.0 * DEPTH) ** 0.25
LN_EPS = 1e-5
RMS_EPS = 1e-6

GLA_QK = H_GLA * DK_GLA
GLA_VW = H_GLA * DV
GDN_QK = H_GDN * DK_GDN
GDN_VW = H_GDN * DV
GDN_CONV_CH = 2 * GDN_QK + GDN_VW
SB_W = H_SB * DH_SB
LIN_MAIN = 2 * GLA_QK + 2 * GLA_VW + 2 * GDN_QK + 2 * GDN_VW
LIN_COLS = LIN_MAIN + 256
SMALL_GA = 0
SMALL_DA = GLA_RANK
SMALL_DB = GLA_RANK + H_GDN

VMEM_LIMIT = 56 * 1024 * 1024


def _cparams(sem):
    return pltpu.CompilerParams(dimension_semantics=sem, vmem_limit_bytes=VMEM_LIMIT)


def _bdot(a, b):
    return jnp.dot(a.astype(BF16), b.astype(BF16), preferred_element_type=F32)


def _bdot_nt(a, b):
    return lax.dot_general(a.astype(BF16), b.astype(BF16), (((1,), (1,)), ((), ())),
                           preferred_element_type=F32)


def _bdot_tn(a, b):
    return lax.dot_general(a.astype(BF16), b.astype(BF16), (((0,), (0,)), ((), ())),
                           preferred_element_type=F32)


def _hilo(x):
    hi = x.astype(BF16)
    lo = (x - hi.astype(F32)).astype(BF16)
    return hi, lo


def _sel_dot_l(m01, x):
    hi, lo = _hilo(x)
    return (jnp.dot(m01, hi, preferred_element_type=F32)
            + jnp.dot(m01, lo, preferred_element_type=F32))


def _sel_dot_r(x, m01):
    hi, lo = _hilo(x)
    return (jnp.dot(hi, m01, preferred_element_type=F32)
            + jnp.dot(lo, m01, preferred_element_type=F32))


def _dot3(a, b, dims=(((1,), (0,)), ((), ()))):
    ah, al = _hilo(a)
    bh, bl = _hilo(b)
    dg = lambda x, y: lax.dot_general(x, y, dims, preferred_element_type=F32)
    return dg(ah, bh) + (dg(ah, bl) + dg(al, bh))


def _softplus(x):
    return jnp.maximum(x, 0.0) + jnp.log1p(jnp.exp(-jnp.abs(x)))


def _silu(x):
    return x * jax.nn.sigmoid(x)


def _layer_norm(r, g, b):
    mu = jnp.mean(r, axis=-1, keepdims=True)
    d = r - mu
    var = jnp.mean(d * d, axis=-1, keepdims=True)
    return d * lax.rsqrt(var + LN_EPS) * g + b


def _iota(shape, dim):
    return lax.broadcasted_iota(jnp.int32, shape, dim)


def _ada_kernel(c_ref, w_ref, b_ref, o_ref):
    a = _silu(c_ref[...]).astype(BF16)
    o_ref[0] = jnp.dot(a, w_ref[0].astype(BF16), preferred_element_type=F32) + b_ref[0]


def ada_all(c_all, w_ada, b_ada, tn=1536):
    nl, d, n = w_ada.shape
    bc = c_all.shape[0]
    return pl.pallas_call(
        _ada_kernel,
        grid=(nl, n // tn),
        in_specs=[pl.BlockSpec((bc, d), lambda l, j: (0, 0)),
                  pl.BlockSpec((1, d, tn), lambda l, j: (l, 0, j)),
                  pl.BlockSpec((1, 1, tn), lambda l, j: (l, 0, j))],
        out_specs=pl.BlockSpec((1, bc, tn), lambda l, j: (l, 0, j)),
        out_shape=jax.ShapeDtypeStruct((nl, bc, n), F32),
        compiler_params=_cparams(("parallel", "parallel")),
        name="ada_mod",
    )(c_all, w_ada, b_ada.reshape(nl, 1, n))


def _modmm_kernel(x_ref, sc_ref, sh_ref, w_ref, o_ref, h_ref):
    @pl.when(pl.program_id(1) == 0)
    def _():
        h_ref[...] = (x_ref[...] * (1.0 + sc_ref[0]) + sh_ref[0]).astype(BF16)

    o_ref[...] = jnp.dot(h_ref[...], w_ref[...], preferred_element_type=F32)


def mod_matmul(x, sc, sh, w, tm, tn):
    t, d = x.shape
    n = w.shape[1]
    nb, r, _ = sc.shape
    tpb = (t // tm) // nb
    mod_spec = pl.BlockSpec((1, r, d), lambda i, j: (i // tpb, 0, 0))
    return pl.pallas_call(
        _modmm_kernel,
        grid=(t // tm, n // tn),
        in_specs=[pl.BlockSpec((tm, d), lambda i, j: (i, 0)), mod_spec, mod_spec,
                  pl.BlockSpec((d, tn), lambda i, j: (0, j))],
        out_specs=pl.BlockSpec((tm, tn), lambda i, j: (i, j)),
        out_shape=jax.ShapeDtypeStruct((t, n), F32),
        scratch_shapes=[pltpu.VMEM((tm, d), BF16)],
        compiler_params=_cparams(("parallel", "arbitrary")),
        name="mod_matmul",
    )(x, sc, sh, w)


def _proj_ln_kernel(n_in, *refs):
    a_refs = refs[:n_in]
    w_refs = refs[n_in:2 * n_in]
    x_ref, g_ref, lg_ref, lb_ref, o_ref = refs[2 * n_in:]
    y = None
    for a_ref, w_ref in zip(a_refs, w_refs):
        p = jnp.dot(a_ref[...].astype(BF16), w_ref[...], preferred_element_type=F32)
        y = p if y is None else y + p
    r = DEEPNORM_ALPHA * x_ref[...] + (1.0 + g_ref[0]) * y
    o_ref[...] = _layer_norm(r, lg_ref[...], lb_ref[...])


def proj_ln(acts, ws, x, gate, ln_g, ln_b, tm):
    t, d = x.shape
    nb, r, _ = gate.shape
    tpb = (t // tm) // nb
    n_in = len(acts)
    in_specs = ([pl.BlockSpec((tm, a.shape[1]), lambda i: (i, 0)) for a in acts]
                + [pl.BlockSpec(w.shape, lambda i: (0, 0)) for w in ws]
                + [pl.BlockSpec((tm, d), lambda i: (i, 0)),
                   pl.BlockSpec((1, r, d), lambda i: (i // tpb, 0, 0)),
                   pl.BlockSpec((1, d), lambda i: (0, 0)),
                   pl.BlockSpec((1, d), lambda i: (0, 0))])
    return pl.pallas_call(
        functools.partial(_proj_ln_kernel, n_in),
        grid=(t // tm,),
        in_specs=in_specs,
        out_specs=pl.BlockSpec((tm, d), lambda i: (i, 0)),
        out_shape=jax.ShapeDtypeStruct((t, d), F32),
        compiler_params=_cparams(("parallel",)),
        name="proj_ln",
    )(*acts, *ws, x, gate, ln_g.reshape(1, d), ln_b.reshape(1, d))


def _gla_kernel(q_ref, k_ref, v_ref, r_ref, sm_ref, wa2_ref, ba_ref, gn_ref, s0_ref,
                o_ref, so_ref, st_ref, *, chunk, sub, rows, valid):
    j = pl.program_id(1)

    @pl.when(j == 0)
    def _():
        st_ref[...] = s0_ref[0]

    c = chunk
    live = min(c, rows)
    nch = max(rows // c, 1)
    nsub = c // sub
    tri = (_iota((c, c), 0) >= _iota((c, c), 1)).astype(BF16)
    lane_head = _iota((1, GLA_QK), 1) // DK_GLA

    def ext(x):
        if live == c:
            return x
        return jnp.concatenate([x, jnp.zeros((c - live, x.shape[1]), x.dtype)], axis=0)

    def stack_heads(x):
        return jnp.concatenate([jnp.where(lane_head == h, x, 0.0) for h in range(H_GLA)], axis=0)

    for ci in range(nch):
        sl = pl.ds(ci * c, live)
        q = ext(q_ref[0, sl, :]) * (DK_GLA ** -0.5)
        k = ext(k_ref[0, sl, :])
        v = ext(v_ref[0, sl, :])
        rg = ext(r_ref[0, sl, :])
        ga = ext(sm_ref[0, sl, :])[:, SMALL_GA:SMALL_GA + GLA_RANK]
        xg = _bdot(ga, wa2_ref[...]) + ba_ref[...]
        la = -_softplus(-xg) * (1.0 / GLA_TAU)
        if valid < c:
            la = jnp.where(_iota((c, GLA_QK), 0) < valid, la, 0.0)
        bc = _sel_dot_l(tri, la)
        st = st_ref[...]
        inter = _bdot_nt(stack_heads(q * jnp.exp(bc)), st)

        parts = [[None] * nsub for _ in range(H_GLA)]
        for si in range(nsub):
            lo = si * sub
            n = lo + sub
            r0 = bc[lo - 1:lo, :] if si > 0 else jnp.zeros((1, GLA_QK), F32)
            qi = q[lo:n] * jnp.exp(bc[lo:n] - r0)
            ki = k[:n] * jnp.exp(r0 - bc[:n])
            att = _bdot_nt(stack_heads(qi), ki)
            rr = _iota((H_GLA * sub, n), 0) & (sub - 1)
            cc = _iota((H_GLA * sub, n), 1) - lo
            att = jnp.where(cc <= rr, att, 0.0)
            ov = _bdot(att, v[:n])
            for h in range(H_GLA):
                parts[h][si] = ov[h * sub:(h + 1) * sub, h * DV:(h + 1) * DV]

        outs = []
        for h in range(H_GLA):
            intra = parts[h][0] if nsub == 1 else jnp.concatenate(parts[h], axis=0)
            o_h = inter[h * c:(h + 1) * c] + intra
            o_h = o_h * lax.rsqrt(jnp.mean(o_h * o_h, axis=-1, keepdims=True) + RMS_EPS) * gn_ref[...]
            outs.append(o_h * _silu(rg[:, h * DV:(h + 1) * DV]))
        o_full = jnp.concatenate(outs, axis=1)
        o_ref[0, sl, :] = o_full[:live]

        bl = bc[c - 1:c, :]
        kh = k * jnp.exp(bl - bc)
        new = st * jnp.exp(bl)
        for h in range(H_GLA):
            upd = _bdot_tn(v[:, h * DV:(h + 1) * DV], kh)
            new = new + jnp.where(lane_head == h, upd, 0.0)
        st_ref[...] = new

    @pl.when(j == pl.num_programs(1) - 1)
    def _():
        so_ref[0] = st_ref[...]


def gla_heads(y3, s0_t, wa2, ba, gn, *, rows, valid, chunk=64, sub=16):
    b, l, _ = y3.shape
    blk = lambda w, idx: pl.BlockSpec((1, rows, w), lambda bi, j: (bi, j, idx))
    const = lambda shp: pl.BlockSpec(shp, lambda bi, j: tuple(0 for _ in shp))
    return pl.pallas_call(
        functools.partial(_gla_kernel, chunk=chunk, sub=min(sub, chunk), rows=rows, valid=valid),
        grid=(b, l // rows),
        in_specs=[blk(GLA_QK, 0), blk(GLA_QK, 1), blk(GLA_VW, 1), blk(GLA_VW, 2),
                  blk(128, LIN_MAIN // 128),
                  const((GLA_RANK, GLA_QK)), const((1, GLA_QK)), const((1, DV)),
                  pl.BlockSpec((1, DV, GLA_QK), lambda bi, j: (bi, 0, 0))],
        out_specs=[pl.BlockSpec((1, rows, GLA_VW), lambda bi, j: (bi, j, 0)),
                   pl.BlockSpec((1, DV, GLA_QK), lambda bi, j: (bi, 0, 0))],
        out_shape=[jax.ShapeDtypeStruct((b, l, GLA_VW), F32),
                   jax.ShapeDtypeStruct((b, DV, GLA_QK), F32)],
        scratch_shapes=[pltpu.VMEM((DV, GLA_QK), F32)],
        compiler_params=_cparams(("parallel", "arbitrary")),
        name="gla_heads",
    )(y3, y3, y3, y3, y3, wa2, ba.reshape(1, GLA_QK), gn.reshape(1, DV), s0_t)


def _gdn_kernel(dq_ref, dk_ref, dv_ref, dz_ref, sm_ref, cw_ref, al_ref, dtb_ref, gn_ref, cp_ref, s0_ref,
                o_ref, so_ref, xx_ref, st_ref, *, chunk, rows, valid):
    j = pl.program_id(1)

    @pl.when(j == 0)
    def _():
        st_ref[...] = s0_ref[0]
        xx_ref[0:8, :] = cp_ref[0]

    c = chunk
    live = min(c, rows)
    nch = max(rows // c, 1)

    xx_ref[8:8 + rows, 0:GDN_QK] = dq_ref[0]
    xx_ref[8:8 + rows, GDN_QK:2 * GDN_QK] = dk_ref[0]
    xx_ref[8:8 + rows, 2 * GDN_QK:GDN_CONV_CH] = dv_ref[0]
    y = None
    for i in range(CONV_W):
        term = xx_ref[pl.ds(8 - (CONV_W - 1) + i, rows), :] * cw_ref[i:i + 1, :]
        y = term if y is None else y + term
    xx_ref[0:8, :] = xx_ref[rows:rows + 8, :]
    qkv = _silu(y)

    def ext(x):
        if live == c:
            return x
        return jnp.concatenate([x, jnp.zeros((c - live, x.shape[1]), x.dtype)], axis=0)

    ri = _iota((c, c), 0)
    cj = _iota((c, c), 1)
    tri = (ri >= cj).astype(BF16)
    incl = ri >= cj
    strict = ri > cj
    strict_f = strict.astype(F32)
    eye = (ri == cj).astype(F32)
    blk16 = (ri >> 4) == (cj >> 4)
    merge_masks = [((ri >> lv) == (cj >> lv)) & ((ri >> (lv - 1)) != (cj >> (lv - 1)))
                   for lv in range(5, int(math.log2(c)) + 1)]
    pick = (_iota((128, 2 * GDN_VW), 0) == SMALL_DA + (_iota((128, 2 * GDN_VW), 1) >> 7)).astype(BF16)
    lane = _iota((c, 128), 1)

    for ci in range(nch):
        sl = pl.ds(ci * c, live)
        blk = ext(qkv[ci * c:ci * c + live])
        dz = ext(dz_ref[0, sl, :])
        sm = ext(sm_ref[0, sl, :])
        g_all = -jnp.exp(al_ref[...]) * _softplus(sm + dtb_ref[...])
        beta_all = jax.nn.sigmoid(sm)
        vals = jnp.where((lane >= SMALL_DA) & (lane < SMALL_DB), g_all,
                         jnp.where((lane >= SMALL_DB) & (lane < SMALL_DB + H_GDN), beta_all, 0.0))
        if valid < c:
            vals = jnp.where(_iota((c, 128), 0) < valid, vals, 0.0)
        gb = _sel_dot_r(vals, pick)
        g_b = gb[:, :GDN_VW]
        gam = _sel_dot_l(tri, g_b)
        egam = jnp.exp(gam)

        outs = []
        for h in range(H_GDN):
            hs = slice(h * DK_GDN, (h + 1) * DK_GDN)
            cq = blk[:, hs]
            ck = blk[:, GDN_QK + h * DK_GDN:GDN_QK + (h + 1) * DK_GDN]
            cv = blk[:, 2 * GDN_QK + h * DV:2 * GDN_QK + (h + 1) * DV]
            qn = cq * lax.rsqrt(jnp.sum(cq * cq, axis=-1, keepdims=True) + RMS_EPS) * (DK_GDN ** -0.5)
            kn = ck * lax.rsqrt(jnp.sum(ck * ck, axis=-1, keepdims=True) + RMS_EPS)
            beta = gb[:, GDN_VW + h * 128:GDN_VW + (h + 1) * 128]
            gh = g_b[:, hs]
            gm = gam[:, hs]
            eg = egam[:, hs]
            gl = gm[c - 1:c, :]
            m = _sel_dot_l(tri, gh * strict_f)
            dec = jnp.where(incl, jnp.exp(m), 0.0)
            kk = _dot3(kn, kn, (((1,), (1,)), ((), ())))
            nm = jnp.where(strict, kk * dec * beta, 0.0)
            rhs = jnp.concatenate([cv * beta, kn * (beta * eg)], axis=1)
            p = jnp.where(blk16, nm, 0.0)
            t = eye - p
            for _ in range(3):
                p = _dot3(p, p)
                t = t + _dot3(t, p)
            for off in merge_masks:
                t = t - _dot3(_dot3(t, jnp.where(off, nm, 0.0)), t)
            x = _dot3(t, rhs)
            u0 = x[:, :DV]
            w = x[:, DV:]
            qk = jnp.where(incl, _bdot_nt(qn, kn) * dec, 0.0)
            s = st_ref[h]
            u = u0 - _bdot(w, s)
            o_h = _bdot(qn * eg, s) + _bdot(qk, u)
            st_ref[h] = s * jnp.exp(gl) + _bdot_tn(kn * jnp.exp(gl - gm), u)
            o_h = o_h * lax.rsqrt(jnp.mean(o_h * o_h, axis=-1, keepdims=True) + RMS_EPS) * gn_ref[...]
            outs.append(o_h * _silu(dz[:, h * DV:(h + 1) * DV]))
        o_full = jnp.concatenate(outs, axis=1)
        o_ref[0, sl, :] = o_full[:live]

    @pl.when(j == pl.num_programs(1) - 1)
    def _():
        so_ref[0] = st_ref[...]


def gdn_heads(y3, s0, conv_prev8, conv_w, alog_row, dtb_row, gn, *, rows, valid, chunk=128):
    b, l, _ = y3.shape
    blk = lambda idx: pl.BlockSpec((1, rows, 512), lambda bi, j: (bi, j, idx))
    const = lambda shp: pl.BlockSpec(shp, lambda bi, j: tuple(0 for _ in shp))
    return pl.pallas_call(
        functools.partial(_gdn_kernel, chunk=chunk, rows=rows, valid=valid),
        grid=(b, l // rows),
        in_specs=[blk(3), blk(4), blk(5), blk(6),
                  pl.BlockSpec((1, rows, 128), lambda bi, j: (bi, j, LIN_MAIN // 128)),
                  const((CONV_W, GDN_CONV_CH)), const((1, 128)), const((1, 128)), const((1, DV)),
                  pl.BlockSpec((1, 8, GDN_CONV_CH), lambda bi, j: (bi, 0, 0)),
                  pl.BlockSpec((1, H_GDN, DK_GDN, DV), lambda bi, j: (bi, 0, 0, 0))],
        out_specs=[pl.BlockSpec((1, rows, GDN_VW), lambda bi, j: (bi, j, 0)),
                   pl.BlockSpec((1, H_GDN, DK_GDN, DV), lambda bi, j: (bi, 0, 0, 0))],
        out_shape=[jax.ShapeDtypeStruct((b, l, GDN_VW), F32),
                   jax.ShapeDtypeStruct((b, H_GDN, DK_GDN, DV), F32)],
        scratch_shapes=[pltpu.VMEM((rows + 8, GDN_CONV_CH), F32),
                        pltpu.VMEM((H_GDN, DK_GDN, DV), F32)],
        compiler_params=_cparams(("parallel", "arbitrary")),
        name="gdn_heads",
    )(y3, y3, y3, y3, y3, conv_w, alog_row, dtb_row, gn.reshape(1, DV), conv_prev8, s0)


def _tri_ones():
    r = _iota((128, 256), 0)
    c = _iota((128, 256), 1)
    return ((r >= c) | (c >= 128)).astype(BF16)


def _sb_chunk(z, mask, to, carry):
    sp = _softplus(z)
    if mask is not None:
        sp = jnp.where(mask, sp, 0.0)
    r_hi = _sel_dot_r(sp[:, 128:], to)
    tot_hi = r_hi[:, :128] + carry
    carry = carry + r_hi[:, 128:]
    r_lo = _sel_dot_r(sp[:, :128], to)
    tot_lo = r_lo[:, :128] + carry
    carry = carry + r_lo[:, 128:]
    a = jnp.exp(z - jnp.concatenate([tot_lo, tot_hi], axis=1))
    if mask is not None:
        a = jnp.where(mask, a, 0.0)
    return a, carry


def _sbp_kernel(q_ref, k_ref, v_ref, b_ref, o_ref, *, tq):
    hp = pl.program_id(1)
    i = pl.program_id(2)
    q = q_ref[0] * (DH_SB ** -0.5)
    lane = _iota((1, 128), 1)
    to = _tri_ones()
    qm = [jnp.where((lane >= DH_SB) == (hh == 1), q, 0.0).astype(BF16) for hh in range(2)]
    bias = [b_ref[pl.ds(2 * hp + hh, 1), :] for hh in range(2)]
    diag_mask = _iota((tq, tq), 1) < _iota((tq, tq), 0)

    def chunk(start, mask, state):
        kb = k_ref[0, pl.ds(start, tq), :].astype(BF16)
        vb = v_ref[0, pl.ds(start, tq), :].astype(BF16)
        new = []
        for hh in range(2):
            carry, acc = state[hh]
            z = lax.dot_general(qm[hh], kb, (((1,), (1,)), ((), ())), preferred_element_type=F32) + bias[hh]
            a, carry = _sb_chunk(z, mask, to, carry)
            acc = acc + jnp.dot(a.astype(BF16), vb, preferred_element_type=F32)
            new.append((carry, acc))
        return tuple(new)

    zero = jnp.zeros((tq, 128), F32)
    state = chunk(pl.multiple_of(i * tq, tq), diag_mask, ((zero, zero), (zero, zero)))

    def body(it, st):
        jj = i - 1 - it
        return chunk(pl.multiple_of(jj * tq, tq), None, st)

    state = lax.fori_loop(0, i, body, state)
    o_ref[0] = jnp.where(lane < DH_SB, state[0][1], state[1][1])


def sb_prompt(y3, bias_rows, tq=256):
    b, l, _ = y3.shape
    nhp = SB_W // 128
    return pl.pallas_call(
        functools.partial(_sbp_kernel, tq=tq),
        grid=(b, nhp, l // tq),
        in_specs=[pl.BlockSpec((1, tq, 128), lambda bi, hp, i: (bi, i, hp)),
                  pl.BlockSpec((1, l, 128), lambda bi, hp, i: (bi, 0, nhp + hp)),
                  pl.BlockSpec((1, l, 128), lambda bi, hp, i: (bi, 0, 2 * nhp + hp)),
                  pl.BlockSpec((H_SB, tq), lambda bi, hp, i: (0, 0))],
        out_specs=pl.BlockSpec((1, tq, 128), lambda bi, hp, i: (bi, i, hp)),
        out_shape=jax.ShapeDtypeStruct((b, l, SB_W), F32),
        compiler_params=_cparams(("parallel", "parallel", "arbitrary")),
        name="sb_prompt",
    )(y3, y3, y3, bias_rows)


def _sbs_kernel(pt_ref, qb_ref, kn_ref, vn_ref, kc_ref, vc_ref, b_ref, o_ref, acc_ref, carry_ref, *, n_tok):
    s = pl.program_id(1)
    rows = n_tok * H_SB
    row = _iota((rows, SB_W), 0)
    lanew = _iota((rows, SB_W), 1)
    qmask = (lanew >> 6) == (row & (H_SB - 1))
    qbd = jnp.where(qmask, qb_ref[0] * (DH_SB ** -0.5), 0.0).astype(BF16)
    to = _tri_ones()

    def process(kblk, vblk, masked):
        z = lax.dot_general(qbd, kblk.astype(BF16), (((1,), (1,)), ((), ())),
                            preferred_element_type=F32) + b_ref[...]
        sp = _softplus(z)
        if masked:
            m = _iota((rows, PAGE_SIZE), 1) < (_iota((rows, PAGE_SIZE), 0) >> 4)
            sp = jnp.where(m, sp, 0.0)
        r = _sel_dot_r(sp, to)
        a = jnp.exp(z - (r[:, :128] + carry_ref[...]))
        if masked:
            a = jnp.where(m, a, 0.0)
        acc_ref[...] += jnp.dot(a.astype(BF16), vblk.astype(BF16), preferred_element_type=F32)
        carry_ref[...] += r[:, 128:]

    @pl.when(s == 0)
    def _():
        acc_ref[...] = jnp.zeros_like(acc_ref)
        carry_ref[...] = jnp.zeros_like(carry_ref)
        process(kn_ref[0], vn_ref[0], True)

    @pl.when(s > 0)
    def _():
        process(kc_ref[0, 0], vc_ref[0, 0], False)

    @pl.when(s == pl.num_programs(1) - 1)
    def _():
        accm = jnp.where(qmask, acc_ref[...], 0.0)
        o_ref[0] = jnp.concatenate(
            [jnp.sum(accm[t * H_SB:(t + 1) * H_SB], axis=0, keepdims=True) for t in range(n_tok)], axis=0)


def sb_sample(q_s, k_new, v_new, cache_k4, cache_v4, page_table, bias_rows, layer):
    b, n_tok, _ = q_s.shape
    npages = page_table.shape[1]
    assert H_SB == 16
    qb = jnp.repeat(q_s, H_SB, axis=1)
    pad = ((0, 0), (0, PAGE_SIZE - n_tok), (0, 0))
    kn = jnp.pad(k_new, pad)
    vn = jnp.pad(v_new, pad)
    rows = n_tok * H_SB

    def page_map(bi, s, pt):
        return (layer, pt[bi, npages - jnp.maximum(s, 1)], 0, 0)

    grid_spec = pltpu.PrefetchScalarGridSpec(
        num_scalar_prefetch=1,
        grid=(b, npages + 1),
        in_specs=[pl.BlockSpec((1, rows, SB_W), lambda bi, s, pt: (bi, 0, 0)),
                  pl.BlockSpec((1, PAGE_SIZE, SB_W), lambda bi, s, pt: (bi, 0, 0)),
                  pl.BlockSpec((1, PAGE_SIZE, SB_W), lambda bi, s, pt: (bi, 0, 0)),
                  pl.BlockSpec((1, 1, PAGE_SIZE, SB_W), page_map),
                  pl.BlockSpec((1, 1, PAGE_SIZE, SB_W), page_map),
                  pl.BlockSpec((rows, 128), lambda bi, s, pt: (0, 0))],
        out_specs=pl.BlockSpec((1, n_tok, SB_W), lambda bi, s, pt: (bi, 0, 0)),
        scratch_shapes=[pltpu.VMEM((rows, SB_W), F32), pltpu.VMEM((rows, 128), F32)],
    )
    return pl.pallas_call(
        functools.partial(_sbs_kernel, n_tok=n_tok),
        grid_spec=grid_spec,
        out_shape=jax.ShapeDtypeStruct((b, n_tok, SB_W), F32),
        compiler_params=_cparams(("parallel", "arbitrary")),
        name="sb_sample",
    )(page_table, qb, kn, vn, cache_k4, cache_v4, bias_rows)


def _router_kernel(x_ref, sc_ref, sh_ref, wrt_ref, eb_ref, o_ref):
    h = x_ref[...] * (1.0 + sc_ref[0]) + sh_ref[0]
    tm = h.shape[0]
    hh, hl = _hilo(h)
    wh, wl = _hilo(wrt_ref[...])
    nt = lambda a, b: lax.dot_general(a, b, (((1,), (1,)), ((), ())), preferred_element_type=F32)
    logits = nt(wh, hh) + nt(wh, hl) + nt(wl, hh)
    s = jax.nn.sigmoid(logits)
    sel = s + eb_ref[...]
    gsz = N_EXPERTS // N_GROUPS
    ninf = -jnp.inf
    g3 = sel.reshape(N_GROUPS, gsz, tm)
    io3 = _iota((N_GROUPS, gsz, tm), 1).astype(F32)
    m1 = jnp.max(g3, axis=1, keepdims=True)
    i1 = jnp.min(jnp.where(g3 == m1, io3, float(gsz)), axis=1, keepdims=True)
    m2 = jnp.max(jnp.where(io3 == i1, ninf, g3), axis=1, keepdims=True)
    gscore = (m1 + m2).reshape(N_GROUPS, tm)
    iog = _iota((N_GROUPS, tm), 0).astype(F32)
    gsel = jnp.zeros((N_GROUPS, tm), F32)
    cur = gscore
    for _ in range(TOPK_GROUPS):
        m = jnp.max(cur, axis=0, keepdims=True)
        idx = jnp.min(jnp.where(cur == m, iog, float(N_GROUPS)), axis=0, keepdims=True)
        hit = iog == idx
        gsel = jnp.where(hit, 1.0, gsel)
        cur = jnp.where(hit, ninf, cur)
    emask = jnp.broadcast_to(gsel.reshape(N_GROUPS, 1, tm), (N_GROUPS, gsz, tm)).reshape(N_EXPERTS, tm)
    cur = jnp.where(emask > 0.0, sel, ninf)
    ioe = _iota((N_EXPERTS, tm), 0).astype(F32)
    chosen = jnp.zeros((N_EXPERTS, tm), F32)
    for _ in range(TOP_K):
        m = jnp.max(cur, axis=0, keepdims=True)
        idx = jnp.min(jnp.where(cur == m, ioe, float(N_EXPERTS)), axis=0, keepdims=True)
        hit = ioe == idx
        chosen = jnp.where(hit, 1.0, chosen)
        cur = jnp.where(hit, ninf, cur)
    wsel = jnp.where(chosen > 0.0, s, 0.0)
    o_ref[...] = wsel / jnp.sum(wsel, axis=0, keepdims=True) * ROUTED_SCALE


def router(x, sc, sh, w_r_t, e_b, tm):
    t, d = x.shape
    nb, r, _ = sc.shape
    tpb = (t // tm) // nb
    mod_spec = pl.BlockSpec((1, r, d), lambda i: (i // tpb, 0, 0))
    return pl.pallas_call(
        _router_kernel,
        grid=(t // tm,),
        in_specs=[pl.BlockSpec((tm, d), lambda i: (i, 0)), mod_spec, mod_spec,
                  pl.BlockSpec((N_EXPERTS, d), lambda i: (0, 0)),
                  pl.BlockSpec((N_EXPERTS, 1), lambda i: (0, 0))],
        out_specs=pl.BlockSpec((N_EXPERTS, tm), lambda i: (0, i)),
        out_shape=jax.ShapeDtypeStruct((N_EXPERTS, t), F32),
        compiler_params=_cparams(("parallel",)),
        name="moe_router",
    )(x, sc, sh, w_r_t, e_b.reshape(N_EXPERTS, 1))


def _moe_kernel(x_ref, sc_ref, sh_ref, ga_ref, gates_ref, wg_ref, wu_ref, wd_ref, sg_ref, su_ref, sd_ref,
                lg_ref, lb_ref, o_ref, hb_ref, acc_ref, gh_ref, gl_ref, *, eps):
    e = pl.program_id(1)
    n_routed_steps = N_EXPERTS // eps

    @pl.when(e == 0)
    def _():
        hb_ref[...] = (x_ref[...] * (1.0 + sc_ref[0]) + sh_ref[0]).astype(BF16)
        acc_ref[...] = jnp.zeros_like(acc_ref)
        g = gates_ref[...]
        gh = g.astype(BF16)
        gh_ref[...] = gh
        gl_ref[...] = (g - gh.astype(F32)).astype(BF16)

    def expert(wg, wu, wd, gcol):
        hb = hb_ref[...]
        g = jnp.dot(hb, wg.astype(BF16), preferred_element_type=F32)
        u = jnp.dot(hb, wu.astype(BF16), preferred_element_type=F32)
        a = _silu(g) * u
        if gcol is not None:
            a = a * gcol
        acc_ref[...] += jnp.dot(a.astype(BF16), wd.astype(BF16), preferred_element_type=F32)

    @pl.when(e < n_routed_steps)
    def _():
        for k in range(eps):
            eid = e * eps + k
            pick = (_iota((N_EXPERTS, D_EXPERT), 0) == eid).astype(BF16)
            gcol = (jnp.dot(gh_ref[...], pick, preferred_element_type=F32)
                    + jnp.dot(gl_ref[...], pick, preferred_element_type=F32))
            expert(wg_ref[k], wu_ref[k], wd_ref[k], gcol)

    @pl.when(e == n_routed_steps)
    def _():
        expert(sg_ref[...], su_ref[...], sd_ref[...], None)
        r = DEEPNORM_ALPHA * x_ref[...] + (1.0 + ga_ref[0]) * acc_ref[...]
        o_ref[...] = _layer_norm(r, lg_ref[...], lb_ref[...])


def moe_ln(x, sc, sh, ga, gates, wg, wu, wd, sg, su, sd, ln_g, ln_b, tm, eps=2):
    t, d = x.shape
    nb, r, _ = sc.shape
    tpb = (t // tm) // nb
    f = wg.shape[2]
    last = N_EXPERTS // eps - 1
    mod_spec = pl.BlockSpec((1, r, d), lambda i, e: (i // tpb, 0, 0))
    const2 = lambda shp: pl.BlockSpec(shp, lambda i, e: (0, 0))
    return pl.pallas_call(
        functools.partial(_moe_kernel, eps=eps),
        grid=(t // tm, N_EXPERTS // eps + 1),
        in_specs=[pl.BlockSpec((tm, d), lambda i, e: (i, 0)), mod_spec, mod_spec, mod_spec,
                  pl.BlockSpec((tm, N_EXPERTS), lambda i, e: (i, 0)),
                  pl.BlockSpec((eps, d, f), lambda i, e: (jnp.minimum(e, last), 0, 0)),
                  pl.BlockSpec((eps, d, f), lambda i, e: (jnp.minimum(e, last), 0, 0)),
                  pl.BlockSpec((eps, f, d), lambda i, e: (jnp.minimum(e, last), 0, 0)),
                  const2((d, f)), const2((d, f)), const2((f, d)), const2((1, d)), const2((1, d))],
        out_specs=pl.BlockSpec((tm, d), lambda i, e: (i, 0)),
        out_shape=jax.ShapeDtypeStruct((t, d), F32),
        scratch_shapes=[pltpu.VMEM((tm, d), BF16), pltpu.VMEM((tm, d), F32),
                        pltpu.VMEM((tm, N_EXPERTS), BF16), pltpu.VMEM((tm, N_EXPERTS), BF16)],
        compiler_params=_cparams(("parallel", "arbitrary")),
        name="moe_experts_ln",
    )(x, sc, sh, ga, gates, wg, wu, wd, sg, su, sd, ln_g.reshape(1, d), ln_b.reshape(1, d))


def _split_mods(m, per_row_repeat):
    parts = jnp.split(m, 6, axis=-1)
    if per_row_repeat is None:
        return [p[:, None, :] for p in parts]
    return [jnp.repeat(p, per_row_repeat, axis=0)[None] for p in parts]


def kernel(x_prompt, x_sample, state_gla, state_gdn, state_conv, cache_k, cache_v, page_table, c_prompt, c_sample, w_ada, b_ada, ln_g, ln_b, w_in_lin, gla_wa2, gla_ba, gla_norm, gdn_conv, gdn_a_log, gdn_dt_bias, gdn_norm, w_out_lin, w_qkv_sb, w_o_sb, sb_bias, w_router, e_bias, w_gate, w_up, w_down, ws_gate, ws_up, ws_down):
    bp, seq, d = x_prompt.shape
    bs, dseq, _ = x_sample.shape
    tp = bp * seq
    ts = bs * dseq
    n_phys = cache_k.shape[1]

    xp = x_prompt.reshape(tp, d)
    xs = x_sample.reshape(ts, d)

    mods = ada_all(jnp.concatenate([c_prompt, c_sample], axis=0), w_ada, b_ada)

    w_lin = jnp.concatenate(
        [w_in_lin[:, :, :1536], w_in_lin[:, :, 1552:3600], w_in_lin[:, :, 1536:1552], w_in_lin[:, :, 3600:3608],
         jnp.zeros((w_in_lin.shape[0], d, LIN_COLS - LIN_MAIN - GLA_RANK - 2 * H_GDN), w_in_lin.dtype)],
        axis=-1).astype(BF16)
    w_out = w_out_lin.astype(BF16)
    w_qkv = w_qkv_sb.astype(BF16)
    w_o = w_o_sb.astype(BF16)
    w_r_t = jnp.swapaxes(w_router, 1, 2)
    cache_k4 = cache_k.reshape(cache_k.shape[0], n_phys, PAGE_SIZE, SB_W)
    cache_v4 = cache_v.reshape(cache_v.shape[0], n_phys, PAGE_SIZE, SB_W)
    tail = jnp.zeros((w_in_lin.shape[0], 1, 128), F32)
    alog_row = tail.at[:, 0, SMALL_DA:SMALL_DB].set(gdn_a_log)
    dtb_row = tail.at[:, 0, SMALL_DA:SMALL_DB].set(gdn_dt_bias)

    pad_rows = 8
    gla_p, gla_s, gdn_p, gdn_s, conv_p, conv_s = [], [], [], [], [], []
    k_p, v_p, k_s, v_s = [], [], [], []
    for layer in range(DEPTH):
        i = layer // 2
        sh_p, sc_p, ga_p, shf_p, scf_p, gaf_p = _split_mods(mods[layer, :bp], None)
        sh_s, sc_s, ga_s, shf_s, scf_s, gaf_s = _split_mods(mods[layer, bp:bp + bs], dseq)
        if layer % 2 == 0:
            yp = mod_matmul(xp, sc_p, sh_p, w_lin[i], tm=512, tn=LIN_COLS // 5)
            ys = mod_matmul(xs, sc_s, sh_s, w_lin[i], tm=ts, tn=LIN_COLS // 5)
            yp3 = yp.reshape(bp, seq, LIN_COLS)
            ys3 = jnp.pad(ys.reshape(bs, dseq, LIN_COLS), ((0, 0), (0, pad_rows - dseq), (0, 0)))
            zeros_t = jnp.zeros((bp, DV, GLA_QK), F32)
            s0_t = state_gla[i].transpose(0, 3, 1, 2).reshape(bs, DV, GLA_QK)
            og_p, st_p = gla_heads(yp3, zeros_t, gla_wa2[i], gla_ba[i], gla_norm[i], rows=256, valid=256)
            og_s, st_s = gla_heads(ys3, s0_t, gla_wa2[i], gla_ba[i], gla_norm[i], rows=pad_rows, valid=dseq)
            gla_p.append(st_p.reshape(bp, DV, H_GLA, DK_GLA).transpose(0, 2, 3, 1))
            gla_s.append(st_s.reshape(bs, DV, H_GLA, DK_GLA).transpose(0, 2, 3, 1))
            cp_p = jnp.zeros((bp, 8, GDN_CONV_CH), F32)
            cp_s = jnp.pad(state_conv[i], ((0, 0), (8 - (CONV_W - 1), 0), (0, 0)))
            od_p, sd_p = gdn_heads(yp3, jnp.zeros((bp, H_GDN, DK_GDN, DV), F32), cp_p, gdn_conv[i],
                                   alog_row[i], dtb_row[i], gdn_norm[i], rows=256, valid=256)
            od_s, sd_s = gdn_heads(ys3, state_gdn[i], cp_s, gdn_conv[i],
                                   alog_row[i], dtb_row[i], gdn_norm[i], rows=pad_rows, valid=dseq)
            gdn_p.append(sd_p)
            gdn_s.append(sd_s)
            c0, c1 = 2 * GLA_QK + 2 * GLA_VW, 2 * GLA_QK + 2 * GLA_VW + GDN_CONV_CH
            conv_p.append(yp3[:, seq - (CONV_W - 1):, c0:c1])
            conv_s.append(jnp.concatenate([state_conv[i], ys.reshape(bs, dseq, LIN_COLS)[:, :, c0:c1]],
                                          axis=1)[:, dseq:])
            xp = proj_ln([og_p.reshape(tp, GLA_VW), od_p.reshape(tp, GDN_VW)],
                         [w_out[i, :GLA_VW], w_out[i, GLA_VW:]], xp, ga_p, ln_g[layer, 0], ln_b[layer, 0], tm=512)
            xs = proj_ln([og_s[:, :dseq].reshape(ts, GLA_VW), od_s[:, :dseq].reshape(ts, GDN_VW)],
                         [w_out[i, :GLA_VW], w_out[i, GLA_VW:]], xs, ga_s, ln_g[layer, 0], ln_b[layer, 0], tm=ts)
        else:
            yp = mod_matmul(xp, sc_p, sh_p, w_qkv[i], tm=512, tn=1024)
            ys = mod_matmul(xs, sc_s, sh_s, w_qkv[i], tm=ts, tn=1024)
            yp3 = yp.reshape(bp, seq, 3 * SB_W)
            ys3 = ys.reshape(bs, dseq, 3 * SB_W)
            bias = sb_bias[i]
            o_p = sb_prompt(yp3, jnp.broadcast_to(bias[:, None], (H_SB, 256)))
            bias_rows = jnp.broadcast_to(jnp.tile(bias, dseq)[:, None], (dseq * H_SB, 128))
            o_s = sb_sample(ys3[:, :, :SB_W], ys3[:, :, SB_W:2 * SB_W], ys3[:, :, 2 * SB_W:],
                            cache_k4, cache_v4, page_table, bias_rows, i)
            k_p.append(yp3[:, :, SB_W:2 * SB_W].reshape(bp, seq, H_SB, DH_SB))
            v_p.append(yp3[:, :, 2 * SB_W:].reshape(bp, seq, H_SB, DH_SB))
            k_s.append(ys3[:, :, SB_W:2 * SB_W].reshape(bs, dseq, H_SB, DH_SB))
            v_s.append(ys3[:, :, 2 * SB_W:].reshape(bs, dseq, H_SB, DH_SB))
            xp = proj_ln([o_p.reshape(tp, SB_W)], [w_o[i]], xp, ga_p, ln_g[layer, 0], ln_b[layer, 0], tm=512)
            xs = proj_ln([o_s.reshape(ts, SB_W)], [w_o[i]], xs, ga_s, ln_g[layer, 0], ln_b[layer, 0], tm=ts)
        mw = (w_gate[layer], w_up[layer], w_down[layer], ws_gate[layer], ws_up[layer], ws_down[layer],
              ln_g[layer, 1], ln_b[layer, 1])
        gates_p = router(xp, scf_p, shf_p, w_r_t[layer], e_bias[layer], tm=512).T
        gates_s = router(xs, scf_s, shf_s, w_r_t[layer], e_bias[layer], tm=ts).T
        xp = moe_ln(xp, scf_p, shf_p, gaf_p, gates_p, *mw, tm=1024)
        xs = moe_ln(xs, scf_s, shf_s, gaf_s, gates_s, *mw, tm=ts)
    return (xp.reshape(bp, seq, d), xs.reshape(bs, dseq, d), jnp.stack(gla_p), jnp.stack(gla_s),
            jnp.stack(gdn_p), jnp.stack(gdn_s), jnp.stack(conv_p), jnp.stack(conv_s),
            jnp.stack(k_p), jnp.stack(v_p), jnp.stack(k_s), jnp.stack(v_s))
```

```python
import functools
import math

import jax
import jax.numpy as jnp
from jax import lax
from jax.experimental import pallas as pl
from jax.experimental.pallas import tpu as pltpu

F32 = jnp.float32
BF16 = jnp.bfloat16

D_MODEL = 1024
DEPTH = 4
PAGE_SIZE = 128
DV = 128
H_GLA = 4
DK_GLA = 64
GLA_RANK = 16
GLA_TAU = 16.0
H_GDN = 4
DK_GDN = 128
CONV_W = 4
H_SB = 16
DH_SB = 64
N_EXPERTS = 64
TOP_K = 8
N_GROUPS = 8
TOPK_GROUPS = 4
D_EXPERT = 256
ROUTED_SCALE = 2.5
DEEPNORM_ALPHA = (2.0 * DEPTH) ** 0.25
LN_EPS = 1e-5
RMS_EPS = 1e-6

GLA_QK = H_GLA * DK_GLA
GLA_VW = H_GLA * DV
GDN_QK = H_GDN * DK_GDN
GDN_VW = H_GDN * DV
GDN_CONV_CH = 2 * GDN_QK + GDN_VW
SB_W = H_SB * DH_SB
LIN_MAIN = 2 * GLA_QK + 2 * GLA_VW + 2 * GDN_QK + 2 * GDN_VW
LIN_COLS = LIN_MAIN + 256
SMALL_GA = 0
SMALL_DA = GLA_RANK
SMALL_DB = GLA_RANK + H_GDN

VMEM_LIMIT = 56 * 1024 * 1024
LOG2E = math.log2(math.e)
SB_UNROLL = 4
MASKED_LOGIT = -1e30


def _cparams(sem):
    return pltpu.CompilerParams(dimension_semantics=sem, vmem_limit_bytes=VMEM_LIMIT)


def _bdot(a, b):
    return jnp.dot(a.astype(BF16), b.astype(BF16), preferred_element_type=F32)


def _bdot_nt(a, b):
    return lax.dot_general(a.astype(BF16), b.astype(BF16), (((1,), (1,)), ((), ())),
                           preferred_element_type=F32)


def _bdot_tn(a, b):
    return lax.dot_general(a.astype(BF16), b.astype(BF16), (((0,), (0,)), ((), ())),
                           preferred_element_type=F32)


def _hilo(x):
    hi = x.astype(BF16)
    lo = (x - hi.astype(F32)).astype(BF16)
    return hi, lo


def _sel_dot_l(m01, x):
    hi, lo = _hilo(x)
    return (jnp.dot(m01, hi, preferred_element_type=F32)
            + jnp.dot(m01, lo, preferred_element_type=F32))


def _sel_dot_r(x, m01):
    hi, lo = _hilo(x)
    return (jnp.dot(hi, m01, preferred_element_type=F32)
            + jnp.dot(lo, m01, preferred_element_type=F32))


def _dot3(a, b, dims=(((1,), (0,)), ((), ()))):
    ah, al = _hilo(a)
    bh, bl = _hilo(b)
    dg = lambda x, y: lax.dot_general(x, y, dims, preferred_element_type=F32)
    return dg(ah, bh) + (dg(ah, bl) + dg(al, bh))


def _softplus(x):
    return jnp.maximum(x, 0.0) + jnp.log1p(jnp.exp(-jnp.abs(x)))


def _silu(x):
    return x * jax.nn.sigmoid(x)


def _layer_norm(r, g, b):
    mu = jnp.mean(r, axis=-1, keepdims=True)
    d = r - mu
    var = jnp.mean(d * d, axis=-1, keepdims=True)
    return d * lax.rsqrt(var + LN_EPS) * g + b


def _iota(shape, dim):
    return lax.broadcasted_iota(jnp.int32, shape, dim)


def _ada_kernel(c_ref, w_ref, b_ref, o_ref):
    a = _silu(c_ref[...]).astype(BF16)
    o_ref[0] = jnp.dot(a, w_ref[0].astype(BF16), preferred_element_type=F32) + b_ref[0]


def ada_all(c_all, w_ada, b_ada, tn=1536):
    nl, d, n = w_ada.shape
    bc = c_all.shape[0]
    return pl.pallas_call(
        _ada_kernel,
        grid=(nl, n // tn),
        in_specs=[pl.BlockSpec((bc, d), lambda l, j: (0, 0)),
                  pl.BlockSpec((1, d, tn), lambda l, j: (l, 0, j)),
                  pl.BlockSpec((1, 1, tn), lambda l, j: (l, 0, j))],
        out_specs=pl.BlockSpec((1, bc, tn), lambda l, j: (l, 0, j)),
        out_shape=jax.ShapeDtypeStruct((nl, bc, n), F32),
        compiler_params=_cparams(("parallel", "parallel")),
        name="ada_mod",
    )(c_all, w_ada, b_ada.reshape(nl, 1, n))


def _modmm_kernel(x_ref, sc_ref, sh_ref, w_ref, o_ref, h_ref):
    @pl.when(pl.program_id(1) == 0)
    def _():
        h_ref[...] = (x_ref[...] * (1.0 + sc_ref[0]) + sh_ref[0]).astype(BF16)

    o_ref[...] = jnp.dot(h_ref[...], w_ref[...], preferred_element_type=F32)


def mod_matmul(x, sc, sh, w, tm, tn):
    t, d = x.shape
    n = w.shape[1]
    nb, r, _ = sc.shape
    tpb = (t // tm) // nb
    mod_spec = pl.BlockSpec((1, r, d), lambda i, j: (i // tpb, 0, 0))
    return pl.pallas_call(
        _modmm_kernel,
        grid=(t // tm, n // tn),
        in_specs=[pl.BlockSpec((tm, d), lambda i, j: (i, 0)), mod_spec, mod_spec,
                  pl.BlockSpec((d, tn), lambda i, j: (0, j))],
        out_specs=pl.BlockSpec((tm, tn), lambda i, j: (i, j)),
        out_shape=jax.ShapeDtypeStruct((t, n), F32),
        scratch_shapes=[pltpu.VMEM((tm, d), BF16)],
        compiler_params=_cparams(("parallel", "arbitrary")),
        name="mod_matmul",
    )(x, sc, sh, w)


def _proj_ln_kernel(n_in, *refs):
    a_refs = refs[:n_in]
    w_refs = refs[n_in:2 * n_in]
    x_ref, g_ref, lg_ref, lb_ref, o_ref = refs[2 * n_in:]
    y = None
    for a_ref, w_ref in zip(a_refs, w_refs):
        p = jnp.dot(a_ref[...].astype(BF16), w_ref[...], preferred_element_type=F32)
        y = p if y is None else y + p
    r = DEEPNORM_ALPHA * x_ref[...] + (1.0 + g_ref[0]) * y
    o_ref[...] = _layer_norm(r, lg_ref[...], lb_ref[...])


def proj_ln(acts, ws, x, gate, ln_g, ln_b, tm):
    t, d = x.shape
    nb, r, _ = gate.shape
    tpb = (t // tm) // nb
    n_in = len(acts)
    in_specs = ([pl.BlockSpec((tm, a.shape[1]), lambda i: (i, 0)) for a in acts]
                + [pl.BlockSpec(w.shape, lambda i: (0, 0)) for w in ws]
                + [pl.BlockSpec((tm, d), lambda i: (i, 0)),
                   pl.BlockSpec((1, r, d), lambda i: (i // tpb, 0, 0)),
                   pl.BlockSpec((1, d), lambda i: (0, 0)),
                   pl.BlockSpec((1, d), lambda i: (0, 0))])
    return pl.pallas_call(
        functools.partial(_proj_ln_kernel, n_in),
        grid=(t // tm,),
        in_specs=in_specs,
        out_specs=pl.BlockSpec((tm, d), lambda i: (i, 0)),
        out_shape=jax.ShapeDtypeStruct((t, d), F32),
        compiler_params=_cparams(("parallel",)),
        name="proj_ln",
    )(*acts, *ws, x, gate, ln_g.reshape(1, d), ln_b.reshape(1, d))


def _gla_kernel(q_ref, k_ref, v_ref, r_ref, sm_ref, wa2_ref, ba_ref, gn_ref, s0_ref,
                o_ref, so_ref, st_ref, *, chunk, sub, rows, valid):
    j = pl.program_id(1)

    @pl.when(j == 0)
    def _():
        st_ref[...] = s0_ref[0]

    c = chunk
    live = min(c, rows)
    nch = max(rows // c, 1)
    nsub = c // sub
    tri = (_iota((c, c), 0) >= _iota((c, c), 1)).astype(BF16)
    lane_head = _iota((1, GLA_QK), 1) // DK_GLA

    def ext(x):
        if live == c:
            return x
        return jnp.concatenate([x, jnp.zeros((c - live, x.shape[1]), x.dtype)], axis=0)

    def stack_heads(x):
        return jnp.concatenate([jnp.where(lane_head == h, x, 0.0) for h in range(H_GLA)], axis=0)

    for ci in range(nch):
        sl = pl.ds(ci * c, live)
        q = ext(q_ref[0, sl, :]) * (DK_GLA ** -0.5)
        k = ext(k_ref[0, sl, :])
        v = ext(v_ref[0, sl, :])
        rg = ext(r_ref[0, sl, :])
        ga = ext(sm_ref[0, sl, :])[:, SMALL_GA:SMALL_GA + GLA_RANK]
        xg = _bdot(ga, wa2_ref[...]) + ba_ref[...]
        la = -_softplus(-xg) * (1.0 / GLA_TAU)
        if valid < c:
            la = jnp.where(_iota((c, GLA_QK), 0) < valid, la, 0.0)
        bc = _sel_dot_l(tri, la)
        st = st_ref[...]
        inter = _bdot_nt(stack_heads(q * jnp.exp(bc)), st)

        parts = [[None] * nsub for _ in range(H_GLA)]
        for si in range(nsub):
            lo = si * sub
            n = lo + sub
            r0 = bc[lo - 1:lo, :] if si > 0 else jnp.zeros((1, GLA_QK), F32)
            qi = q[lo:n] * jnp.exp(bc[lo:n] - r0)
            ki = k[:n] * jnp.exp(r0 - bc[:n])
            att = _bdot_nt(stack_heads(qi), ki)
            rr = _iota((H_GLA * sub, n), 0) & (sub - 1)
            cc = _iota((H_GLA * sub, n), 1) - lo
            att = jnp.where(cc <= rr, att, 0.0)
            ov = _bdot(att, v[:n])
            for h in range(H_GLA):
                parts[h][si] = ov[h * sub:(h + 1) * sub, h * DV:(h + 1) * DV]

        outs = []
        for h in range(H_GLA):
            intra = parts[h][0] if nsub == 1 else jnp.concatenate(parts[h], axis=0)
            o_h = inter[h * c:(h + 1) * c] + intra
            o_h = o_h * lax.rsqrt(jnp.mean(o_h * o_h, axis=-1, keepdims=True) + RMS_EPS) * gn_ref[...]
            outs.append(o_h * _silu(rg[:, h * DV:(h + 1) * DV]))
        o_full = jnp.concatenate(outs, axis=1)
        o_ref[0, sl, :] = o_full[:live]

        bl = bc[c - 1:c, :]
        kh = k * jnp.exp(bl - bc)
        new = st * jnp.exp(bl)
        for h in range(H_GLA):
            upd = _bdot_tn(v[:, h * DV:(h + 1) * DV], kh)
            new = new + jnp.where(lane_head == h, upd, 0.0)
        st_ref[...] = new

    @pl.when(j == pl.num_programs(1) - 1)
    def _():
        so_ref[0] = st_ref[...]


def gla_heads(y3, s0_t, wa2, ba, gn, *, rows, valid, chunk=64, sub=16):
    b, l, _ = y3.shape
    blk = lambda w, idx: pl.BlockSpec((1, rows, w), lambda bi, j: (bi, j, idx))
    const = lambda shp: pl.BlockSpec(shp, lambda bi, j: tuple(0 for _ in shp))
    return pl.pallas_call(
        functools.partial(_gla_kernel, chunk=chunk, sub=min(sub, chunk), rows=rows, valid=valid),
        grid=(b, l // rows),
        in_specs=[blk(GLA_QK, 0), blk(GLA_QK, 1), blk(GLA_VW, 1), blk(GLA_VW, 2),
                  blk(128, LIN_MAIN // 128),
                  const((GLA_RANK, GLA_QK)), const((1, GLA_QK)), const((1, DV)),
                  pl.BlockSpec((1, DV, GLA_QK), lambda bi, j: (bi, 0, 0))],
        out_specs=[pl.BlockSpec((1, rows, GLA_VW), lambda bi, j: (bi, j, 0)),
                   pl.BlockSpec((1, DV, GLA_QK), lambda bi, j: (bi, 0, 0))],
        out_shape=[jax.ShapeDtypeStruct((b, l, GLA_VW), F32),
                   jax.ShapeDtypeStruct((b, DV, GLA_QK), F32)],
        scratch_shapes=[pltpu.VMEM((DV, GLA_QK), F32)],
        compiler_params=_cparams(("parallel", "arbitrary")),
        name="gla_heads",
    )(y3, y3, y3, y3, y3, wa2, ba.reshape(1, GLA_QK), gn.reshape(1, DV), s0_t)


def _gdn_kernel(dq_ref, dk_ref, dv_ref, dz_ref, sm_ref, cw_ref, al_ref, dtb_ref, gn_ref, cp_ref, s0_ref,
                o_ref, so_ref, xx_ref, st_ref, *, chunk, rows, valid):
    j = pl.program_id(1)

    @pl.when(j == 0)
    def _():
        st_ref[...] = s0_ref[0]
        xx_ref[0:8, :] = cp_ref[0]

    c = chunk
    live = min(c, rows)
    nch = max(rows // c, 1)

    xx_ref[8:8 + rows, 0:GDN_QK] = dq_ref[0]
    xx_ref[8:8 + rows, GDN_QK:2 * GDN_QK] = dk_ref[0]
    xx_ref[8:8 + rows, 2 * GDN_QK:GDN_CONV_CH] = dv_ref[0]
    y = None
    for i in range(CONV_W):
        term = xx_ref[pl.ds(8 - (CONV_W - 1) + i, rows), :] * cw_ref[i:i + 1, :]
        y = term if y is None else y + term
    xx_ref[0:8, :] = xx_ref[rows:rows + 8, :]
    qkv = _silu(y)

    def ext(x):
        if live == c:
            return x
        return jnp.concatenate([x, jnp.zeros((c - live, x.shape[1]), x.dtype)], axis=0)

    ri = _iota((c, c), 0)
    cj = _iota((c, c), 1)
    tri = (ri >= cj).astype(BF16)
    incl = ri >= cj
    strict = ri > cj
    strict_f = strict.astype(F32)
    eye = (ri == cj).astype(F32)
    blk16 = (ri >> 4) == (cj >> 4)
    merge_masks = [((ri >> lv) == (cj >> lv)) & ((ri >> (lv - 1)) != (cj >> (lv - 1)))
                   for lv in range(5, int(math.log2(c)) + 1)]
    pick = (_iota((128, 2 * GDN_VW), 0) == SMALL_DA + (_iota((128, 2 * GDN_VW), 1) >> 7)).astype(BF16)
    lane = _iota((c, 128), 1)

    items = []
    for ci in range(nch):
        sl = pl.ds(ci * c, live)
        blk = ext(qkv[ci * c:ci * c + live])
        dz = ext(dz_ref[0, sl, :])
        sm = ext(sm_ref[0, sl, :])
        g_all = -jnp.exp(al_ref[...]) * _softplus(sm + dtb_ref[...])
        beta_all = jax.nn.sigmoid(sm)
        vals = jnp.where((lane >= SMALL_DA) & (lane < SMALL_DB), g_all,
                         jnp.where((lane >= SMALL_DB) & (lane < SMALL_DB + H_GDN), beta_all, 0.0))
        if valid < c:
            vals = jnp.where(_iota((c, 128), 0) < valid, vals, 0.0)
        gb = _sel_dot_r(vals, pick)
        g_b = gb[:, :GDN_VW]
        gam = _sel_dot_l(tri, g_b)
        egam = jnp.exp(gam)

        for h in range(H_GDN):
            hs = slice(h * DK_GDN, (h + 1) * DK_GDN)
            cq = blk[:, hs]
            ck = blk[:, GDN_QK + h * DK_GDN:GDN_QK + (h + 1) * DK_GDN]
            cv = blk[:, 2 * GDN_QK + h * DV:2 * GDN_QK + (h + 1) * DV]
            qn = cq * lax.rsqrt(jnp.sum(cq * cq, axis=-1, keepdims=True) + RMS_EPS) * (DK_GDN ** -0.5)
            kn = ck * lax.rsqrt(jnp.sum(ck * ck, axis=-1, keepdims=True) + RMS_EPS)
            beta = gb[:, GDN_VW + h * 128:GDN_VW + (h + 1) * 128]
            gm = gam[:, hs]
            eg = egam[:, hs]
            items.append(dict(ci=ci, h=h, sl=sl, qn=qn, kn=kn, beta=beta, gh=g_b[:, hs], gm=gm, eg=eg,
                              gl=gm[c - 1:c, :],
                              rhs=jnp.concatenate([cv * beta, kn * (beta * eg)], axis=1),
                              dz=dz[:, h * DV:(h + 1) * DV]))

    for it in items:
        it["dec"] = jnp.where(incl, jnp.exp(_sel_dot_l(tri, it["gh"] * strict_f)), 0.0)
        it["kk"] = _dot3(it["kn"], it["kn"], (((1,), (1,)), ((), ())))
    for it in items:
        it["nm"] = jnp.where(strict, it["kk"] * it["dec"] * it["beta"], 0.0)
        it["p"] = jnp.where(blk16, it["nm"], 0.0)
        it["t"] = eye - it["p"]
    for _ in range(3):
        for it in items:
            it["p"] = _dot3(it["p"], it["p"])
        for it in items:
            it["t"] = it["t"] + _dot3(it["t"], it["p"])
    for off in merge_masks:
        for it in items:
            it["tc"] = _dot3(it["t"], jnp.where(off, it["nm"], 0.0))
        for it in items:
            it["t"] = it["t"] - _dot3(it["tc"], it["t"])
    for it in items:
        it["x"] = _dot3(it["t"], it["rhs"])
        it["qk"] = jnp.where(incl, _bdot_nt(it["qn"], it["kn"]) * it["dec"], 0.0)

    state = [st_ref[h] for h in range(H_GDN)]
    for ci in range(nch):
        outs = []
        for it in items[ci * H_GDN:(ci + 1) * H_GDN]:
            h, s = it["h"], state[it["h"]]
            u = it["x"][:, :DV] - _bdot(it["x"][:, DV:], s)
            o_h = _bdot(it["qn"] * it["eg"], s) + _bdot(it["qk"], u)
            state[h] = s * jnp.exp(it["gl"]) + _bdot_tn(it["kn"] * jnp.exp(it["gl"] - it["gm"]), u)
            o_h = o_h * lax.rsqrt(jnp.mean(o_h * o_h, axis=-1, keepdims=True) + RMS_EPS) * gn_ref[...]
            outs.append(o_h * _silu(it["dz"]))
        o_full = jnp.concatenate(outs, axis=1)
        o_ref[0, items[ci * H_GDN]["sl"], :] = o_full[:live]
    for h in range(H_GDN):
        st_ref[h] = state[h]

    @pl.when(j == pl.num_programs(1) - 1)
    def _():
        so_ref[0] = st_ref[...]


def gdn_heads(y3, s0, conv_prev8, conv_w, alog_row, dtb_row, gn, *, rows, valid, chunk=128):
    b, l, _ = y3.shape
    blk = lambda idx: pl.BlockSpec((1, rows, 512), lambda bi, j: (bi, j, idx))
    const = lambda shp: pl.BlockSpec(shp, lambda bi, j: tuple(0 for _ in shp))
    return pl.pallas_call(
        functools.partial(_gdn_kernel, chunk=chunk, rows=rows, valid=valid),
        grid=(b, l // rows),
        in_specs=[blk(3), blk(4), blk(5), blk(6),
                  pl.BlockSpec((1, rows, 128), lambda bi, j: (bi, j, LIN_MAIN // 128)),
                  const((CONV_W, GDN_CONV_CH)), const((1, 128)), const((1, 128)), const((1, DV)),
                  pl.BlockSpec((1, 8, GDN_CONV_CH), lambda bi, j: (bi, 0, 0)),
                  pl.BlockSpec((1, H_GDN, DK_GDN, DV), lambda bi, j: (bi, 0, 0, 0))],
        out_specs=[pl.BlockSpec((1, rows, GDN_VW), lambda bi, j: (bi, j, 0)),
                   pl.BlockSpec((1, H_GDN, DK_GDN, DV), lambda bi, j: (bi, 0, 0, 0))],
        out_shape=[jax.ShapeDtypeStruct((b, l, GDN_VW), F32),
                   jax.ShapeDtypeStruct((b, H_GDN, DK_GDN, DV), F32)],
        scratch_shapes=[pltpu.VMEM((rows + 8, GDN_CONV_CH), F32),
                        pltpu.VMEM((H_GDN, DK_GDN, DV), F32)],
        compiler_params=_cparams(("parallel", "arbitrary")),
        name="gdn_heads",
    )(y3, y3, y3, y3, y3, conv_w, alog_row, dtb_row, gn.reshape(1, DV), conv_prev8, s0)


def _tri_ones():
    r = _iota((128, 256), 0)
    c = _iota((128, 256), 1)
    return ((r >= c) | (c >= 128)).astype(BF16)


def _sbp_kernel(q_ref, k_ref, v_ref, b_ref, o_ref, *, tq):
    hp = pl.program_id(1)
    i = pl.program_id(2)
    q = q_ref[0] * (DH_SB ** -0.5 * LOG2E)
    lane = _iota((1, 128), 1)
    to = _tri_ones()
    qm = [jnp.where((lane >= DH_SB) == (hh == 1), q, 0.0).astype(BF16) for hh in range(2)]
    bias = [b_ref[pl.ds(2 * hp + hh, 1), :] * LOG2E for hh in range(2)]
    diag_mask = _iota((tq, tq), 1) < _iota((tq, tq), 0)

    def rows_of(n):
        return pl.ds(pl.multiple_of(jnp.clip(i - n, 0, i) * tq, tq), tq)

    def logits(n, mask):
        kb = k_ref[0, rows_of(n), :].astype(BF16)
        off = jnp.where(n > i, MASKED_LOGIT, 0.0)
        zs = []
        for hh in range(2):
            z = lax.dot_general(qm[hh], kb, (((1,), (1,)), ((), ())), preferred_element_type=F32) + (bias[hh] + off)
            if mask is not None:
                z = jnp.where(mask, z, MASKED_LOGIT)
            zs.append(z)
        return tuple(zs)

    def suffix(zs):
        spbs = [(jnp.maximum(z, 0.0) + jnp.log2(1.0 + jnp.exp2(-jnp.abs(z)))).astype(BF16) for z in zs]
        r_hi = [jnp.dot(spb[:, 128:], to, preferred_element_type=F32) for spb in spbs]
        r_lo = [jnp.dot(spb[:, :128], to, preferred_element_type=F32) for spb in spbs]
        outs = []
        for z, hi, lo in zip(zs, r_hi, r_lo):
            totl = jnp.concatenate([lo[:, :128] + hi[:, 128:], hi[:, :128]], axis=1)
            outs.append((z - totl, lo[:, 128:] + hi[:, 128:]))
        return tuple(outs)

    def weigh(n, ds, state):
        carries, acc = state
        vb = v_ref[0, rows_of(n), :].astype(BF16)
        vcat = jnp.concatenate([jnp.where(lane < DH_SB, vb, 0), jnp.where(lane < DH_SB, 0, vb)], axis=0)
        a = [jnp.exp2(d - jnp.concatenate([c, c], axis=1)).astype(BF16) for (d, _), c in zip(ds, carries)]
        acc = acc + jnp.dot(jnp.concatenate(a, axis=1), vcat, preferred_element_type=F32)
        return tuple(c + rs for (_, rs), c in zip(ds, carries)), acc

    zero = jnp.where(lane < 0, q, 0.0)
    z0 = logits(0, diag_mask)
    idle = tuple((jnp.minimum(z, MASKED_LOGIT), zero) for z in z0)

    def step(t, carried):
        zs, ds, st = carried
        return logits(t + 1, None), suffix(zs), weigh(t - 1, ds, st)

    def body(tu, carried):
        for u in range(SB_UNROLL):
            carried = step(SB_UNROLL * tu + u, carried)
        return carried

    trips = lax.shift_right_logical(i + 2 + SB_UNROLL - 1, SB_UNROLL.bit_length() - 1)
    _, _, state = lax.fori_loop(0, trips, body, (z0, idle, ((zero, zero), zero)))
    o_ref[0] = state[1]


def sb_prompt(y3, bias_rows, tq=256):
    b, l, _ = y3.shape
    nhp = SB_W // 128
    return pl.pallas_call(
        functools.partial(_sbp_kernel, tq=tq),
        grid=(b, nhp, l // tq),
        in_specs=[pl.BlockSpec((1, tq, 128), lambda bi, hp, i: (bi, i, hp)),
                  pl.BlockSpec((1, l, 128), lambda bi, hp, i: (bi, 0, nhp + hp)),
                  pl.BlockSpec((1, l, 128), lambda bi, hp, i: (bi, 0, 2 * nhp + hp)),
                  pl.BlockSpec((H_SB, tq), lambda bi, hp, i: (0, 0))],
        out_specs=pl.BlockSpec((1, tq, 128), lambda bi, hp, i: (bi, i, hp)),
        out_shape=jax.ShapeDtypeStruct((b, l, SB_W), F32),
        compiler_params=_cparams(("parallel", "parallel", "arbitrary")),
        name="sb_prompt",
    )(y3, y3, y3, bias_rows)


def _sbs_kernel(pt_ref, q_ref, kn_ref, vn_ref, *rest, n_tok, group):
    k_refs = rest[:group]
    v_refs = rest[group:2 * group]
    b_ref, o_ref, acc_ref, carry_ref = rest[2 * group:]
    s = pl.program_id(1)
    rows = n_tok * H_SB
    own_head = (_iota((rows, SB_W), 1) >> 6) == (_iota((rows, SB_W), 0) & (H_SB - 1))
    qbd = jnp.where(own_head, q_ref[0] * (DH_SB ** -0.5 * LOG2E), 0.0).astype(BF16)
    causal_new = _iota((rows, PAGE_SIZE), 1) < (_iota((rows, PAGE_SIZE), 0) >> 4)
    to = _tri_ones()
    bias = b_ref[...]

    def sweep(kts, vts, keep):
        zs = [jnp.dot(qbd, kt.astype(BF16), preferred_element_type=F32) + bias for kt in kts]
        if keep is not None:
            zs = [jnp.where(keep, z, MASKED_LOGIT) for z in zs]
        sps = [(jnp.maximum(z, 0.0) + jnp.log2(1.0 + jnp.exp2(-jnp.abs(z)))).astype(BF16) for z in zs]
        rs = [jnp.dot(sp, to, preferred_element_type=F32) for sp in sps]
        carry = carry_ref[...]
        a = []
        for z, r in zip(zs, rs):
            a.append(jnp.exp2(z - (r[:, :128] + carry)).astype(BF16))
            carry = carry + r[:, 128:]
        carry_ref[...] = carry
        a_cat = a[0] if len(a) == 1 else jnp.concatenate(a, axis=1)
        vbs = [vt.astype(BF16) for vt in vts]
        v_cat = vbs[0] if len(vbs) == 1 else jnp.concatenate(vbs, axis=1)
        acc_ref[...] += lax.dot_general(a_cat, v_cat, (((1,), (1,)), ((), ())),
                                        preferred_element_type=F32)

    @pl.when(s == 0)
    def _():
        acc_ref[...] = jnp.zeros_like(acc_ref)
        carry_ref[...] = jnp.zeros_like(carry_ref)
        sweep([kn_ref[0]], [vn_ref[0]], causal_new)

    @pl.when(s > 0)
    def _():
        sweep([k_refs[g][0, 0] for g in range(group)], [v_refs[g][0, 0] for g in range(group)], None)

    @pl.when(s == pl.num_programs(1) - 1)
    def _():
        accm = jnp.where(own_head, acc_ref[...], 0.0)
        o_ref[0] = jnp.concatenate(
            [jnp.sum(accm[t * H_SB:(t + 1) * H_SB], axis=0, keepdims=True) for t in range(n_tok)], axis=0)


def sb_sample(q_s, k_new, v_new, cache_k, cache_v, page_table, bias, layer, group=4):
    b, n_tok, _, _ = q_s.shape
    nl, n_phys = cache_k.shape[:2]
    npages = page_table.shape[1]
    assert H_SB == 16 and npages % group == 0 and n_tok * H_SB <= 128
    rows = n_tok * H_SB
    ck = cache_k.transpose(0, 1, 3, 4, 2).reshape(nl, n_phys, SB_W, PAGE_SIZE)
    cv = cache_v.transpose(0, 1, 3, 4, 2).reshape(nl, n_phys, SB_W, PAGE_SIZE)
    pad = ((0, 0), (0, 0), (0, PAGE_SIZE - n_tok))
    kn = jnp.pad(k_new.reshape(b, n_tok, SB_W).transpose(0, 2, 1), pad)
    vn = jnp.pad(v_new.reshape(b, n_tok, SB_W).transpose(0, 2, 1), pad)
    qrep = jnp.repeat(q_s.reshape(b, n_tok, SB_W), H_SB, axis=1)
    bias_rows = jnp.broadcast_to((jnp.tile(bias, n_tok) * LOG2E)[:, None], (rows, PAGE_SIZE))

    def page_map(k):
        return lambda bi, s, pt: (layer, pt[bi, npages - 1 - (jnp.maximum(s, 1) - 1) * group - k], 0, 0)

    page_specs = [pl.BlockSpec((1, 1, SB_W, PAGE_SIZE), page_map(k)) for k in range(group)]
    grid_spec = pltpu.PrefetchScalarGridSpec(
        num_scalar_prefetch=1,
        grid=(b, npages // group + 1),
        in_specs=[pl.BlockSpec((1, rows, SB_W), lambda bi, s, pt: (bi, 0, 0)),
                  pl.BlockSpec((1, SB_W, PAGE_SIZE), lambda bi, s, pt: (bi, 0, 0)),
                  pl.BlockSpec((1, SB_W, PAGE_SIZE), lambda bi, s, pt: (bi, 0, 0))]
                 + page_specs + page_specs
                 + [pl.BlockSpec((rows, PAGE_SIZE), lambda bi, s, pt: (0, 0))],
        out_specs=pl.BlockSpec((1, n_tok, SB_W), lambda bi, s, pt: (bi, 0, 0)),
        scratch_shapes=[pltpu.VMEM((rows, SB_W), F32), pltpu.VMEM((rows, PAGE_SIZE), F32)],
    )
    out = pl.pallas_call(
        functools.partial(_sbs_kernel, n_tok=n_tok, group=group),
        grid_spec=grid_spec,
        out_shape=jax.ShapeDtypeStruct((b, n_tok, SB_W), F32),
        compiler_params=_cparams(("parallel", "arbitrary")),
        name="sb_sample",
    )(page_table, qrep, kn, vn, *([ck] * group), *([cv] * group), bias_rows)
    return out.reshape(b, n_tok, H_SB, DH_SB)


def _router_kernel(x_ref, sc_ref, sh_ref, wrt_ref, eb_ref, o_ref):
    h = x_ref[...] * (1.0 + sc_ref[0]) + sh_ref[0]
    tm = h.shape[0]
    hh, hl = _hilo(h)
    wh, wl = _hilo(wrt_ref[...])
    nt = lambda a, b: lax.dot_general(a, b, (((1,), (1,)), ((), ())), preferred_element_type=F32)
    logits = nt(wh, hh) + nt(wh, hl) + nt(wl, hh)
    s = jax.nn.sigmoid(logits)
    sel = s + eb_ref[...]
    gsz = N_EXPERTS // N_GROUPS
    ninf = -jnp.inf
    g3 = sel.reshape(N_GROUPS, gsz, tm)
    io3 = _iota((N_GROUPS, gsz, tm), 1).astype(F32)
    m1 = jnp.max(g3, axis=1, keepdims=True)
    i1 = jnp.min(jnp.where(g3 == m1, io3, float(gsz)), axis=1, keepdims=True)
    m2 = jnp.max(jnp.where(io3 == i1, ninf, g3), axis=1, keepdims=True)
    gscore = (m1 + m2).reshape(N_GROUPS, tm)
    iog = _iota((N_GROUPS, tm), 0).astype(F32)
    gsel = jnp.zeros((N_GROUPS, tm), F32)
    cur = gscore
    for _ in range(TOPK_GROUPS):
        m = jnp.max(cur, axis=0, keepdims=True)
        idx = jnp.min(jnp.where(cur == m, iog, float(N_GROUPS)), axis=0, keepdims=True)
        hit = iog == idx
        gsel = jnp.where(hit, 1.0, gsel)
        cur = jnp.where(hit, ninf, cur)
    emask = jnp.broadcast_to(gsel.reshape(N_GROUPS, 1, tm), (N_GROUPS, gsz, tm)).reshape(N_EXPERTS, tm)
    cur = jnp.where(emask > 0.0, sel, ninf)
    ioe = _iota((N_EXPERTS, tm), 0).astype(F32)
    chosen = jnp.zeros((N_EXPERTS, tm), F32)
    for _ in range(TOP_K):
        m = jnp.max(cur, axis=0, keepdims=True)
        idx = jnp.min(jnp.where(cur == m, ioe, float(N_EXPERTS)), axis=0, keepdims=True)
        hit = ioe == idx
        chosen = jnp.where(hit, 1.0, chosen)
        cur = jnp.where(hit, ninf, cur)
    wsel = jnp.where(chosen > 0.0, s, 0.0)
    o_ref[...] = wsel / jnp.sum(wsel, axis=0, keepdims=True) * ROUTED_SCALE


def router(x, sc, sh, w_r_t, e_b, tm):
    t, d = x.shape
    nb, r, _ = sc.shape
    tpb = (t // tm) // nb
    mod_spec = pl.BlockSpec((1, r, d), lambda i: (i // tpb, 0, 0))
    return pl.pallas_call(
        _router_kernel,
        grid=(t // tm,),
        in_specs=[pl.BlockSpec((tm, d), lambda i: (i, 0)), mod_spec, mod_spec,
                  pl.BlockSpec((N_EXPERTS, d), lambda i: (0, 0)),
                  pl.BlockSpec((N_EXPERTS, 1), lambda i: (0, 0))],
        out_specs=pl.BlockSpec((N_EXPERTS, tm), lambda i: (0, i)),
        out_shape=jax.ShapeDtypeStruct((N_EXPERTS, t), F32),
        compiler_params=_cparams(("parallel",)),
        name="moe_router",
    )(x, sc, sh, w_r_t, e_b.reshape(N_EXPERTS, 1))


def _moe_kernel(x_ref, sc_ref, sh_ref, ga_ref, gates_ref, wg_ref, wu_ref, wd_ref, sg_ref, su_ref, sd_ref,
                lg_ref, lb_ref, o_ref, hb_ref, acc_ref, g2_ref, *, eps):
    e = pl.program_id(1)
    n_routed_steps = N_EXPERTS // eps

    @pl.when(e == 0)
    def _():
        hb_ref[...] = (x_ref[...] * (1.0 + sc_ref[0]) + sh_ref[0]).astype(BF16)
        acc_ref[...] = jnp.zeros_like(acc_ref)
        g = gates_ref[...]
        gh = g.astype(BF16)
        g2_ref[:, :N_EXPERTS] = gh
        g2_ref[:, N_EXPERTS:] = (g - gh.astype(F32)).astype(BF16)

    def expert(wg, wu, wd, gcol):
        hb = hb_ref[...]
        g = jnp.dot(hb, wg.astype(BF16), preferred_element_type=F32)
        u = jnp.dot(hb, wu.astype(BF16), preferred_element_type=F32)
        a = _silu(g) * u
        if gcol is not None:
            a = a * gcol
        acc_ref[...] += jnp.dot(a.astype(BF16), wd.astype(BF16), preferred_element_type=F32)

    @pl.when(e < n_routed_steps)
    def _():
        for k in range(eps):
            eid = e * eps + k
            pick = ((_iota((2 * N_EXPERTS, D_EXPERT), 0) & (N_EXPERTS - 1)) == eid).astype(BF16)
            gcol = jnp.dot(g2_ref[...], pick, preferred_element_type=F32)
            expert(wg_ref[k], wu_ref[k], wd_ref[k], gcol)

    @pl.when(e == n_routed_steps)
    def _():
        expert(sg_ref[...], su_ref[...], sd_ref[...], None)
        r = DEEPNORM_ALPHA * x_ref[...] + (1.0 + ga_ref[0]) * acc_ref[...]
        o_ref[...] = _layer_norm(r, lg_ref[...], lb_ref[...])


def moe_ln(x, sc, sh, ga, gates, wg, wu, wd, sg, su, sd, ln_g, ln_b, layer, tm, eps=2):
    t, d = x.shape
    nb, r, _ = sc.shape
    tpb = (t // tm) // nb
    f = wg.shape[3]
    last = N_EXPERTS // eps - 1
    mod_spec = pl.BlockSpec((1, r, d), lambda i, e: (i // tpb, 0, 0))
    const2 = lambda shp: pl.BlockSpec(shp, lambda i, e: (0, 0))
    routed = lambda a, b: pl.BlockSpec((None, eps, a, b), lambda i, e: (layer, jnp.minimum(e, last), 0, 0))
    shared = lambda a, b: pl.BlockSpec((None, a, b), lambda i, e: (layer, 0, 0))
    return pl.pallas_call(
        functools.partial(_moe_kernel, eps=eps),
        grid=(t // tm, N_EXPERTS // eps + 1),
        in_specs=[pl.BlockSpec((tm, d), lambda i, e: (i, 0)), mod_spec, mod_spec, mod_spec,
                  pl.BlockSpec((tm, N_EXPERTS), lambda i, e: (i, 0)),
                  routed(d, f), routed(d, f), routed(f, d),
                  shared(d, f), shared(d, f), shared(f, d), const2((1, d)), const2((1, d))],
        out_specs=pl.BlockSpec((tm, d), lambda i, e: (i, 0)),
        out_shape=jax.ShapeDtypeStruct((t, d), F32),
        scratch_shapes=[pltpu.VMEM((tm, d), BF16), pltpu.VMEM((tm, d), F32),
                        pltpu.VMEM((tm, 2 * N_EXPERTS), BF16)],
        compiler_params=_cparams(("parallel", "arbitrary")),
        name="moe_experts_ln",
    )(x, sc, sh, ga, gates, wg, wu, wd, sg, su, sd, ln_g.reshape(1, d), ln_b.reshape(1, d))


def _split_mods(m, per_row_repeat):
    parts = jnp.split(m, 6, axis=-1)
    if per_row_repeat is None:
        return [p[:, None, :] for p in parts]
    return [jnp.repeat(p, per_row_repeat, axis=0)[None] for p in parts]


def kernel(x_prompt, x_sample, state_gla, state_gdn, state_conv, cache_k, cache_v, page_table, c_prompt, c_sample, w_ada, b_ada, ln_g, ln_b, w_in_lin, gla_wa2, gla_ba, gla_norm, gdn_conv, gdn_a_log, gdn_dt_bias, gdn_norm, w_out_lin, w_qkv_sb, w_o_sb, sb_bias, w_router, e_bias, w_gate, w_up, w_down, ws_gate, ws_up, ws_down):
    bp, seq, d = x_prompt.shape
    bs, dseq, _ = x_sample.shape
    tp = bp * seq
    ts = bs * dseq
    n_phys = cache_k.shape[1]

    xp = x_prompt.reshape(tp, d)
    xs = x_sample.reshape(ts, d)

    mods = ada_all(jnp.concatenate([c_prompt, c_sample], axis=0), w_ada, b_ada)

    w_lin = jnp.concatenate(
        [w_in_lin[:, :, :1536], w_in_lin[:, :, 1552:3600], w_in_lin[:, :, 1536:1552], w_in_lin[:, :, 3600:3608],
         jnp.zeros((w_in_lin.shape[0], d, LIN_COLS - LIN_MAIN - GLA_RANK - 2 * H_GDN), w_in_lin.dtype)],
        axis=-1).astype(BF16)
    w_out = w_out_lin.astype(BF16)
    w_qkv = w_qkv_sb.astype(BF16)
    w_o = w_o_sb.astype(BF16)
    w_r_t = jnp.swapaxes(w_router, 1, 2)
    tail = jnp.zeros((w_in_lin.shape[0], 1, 128), F32)
    alog_row = tail.at[:, 0, SMALL_DA:SMALL_DB].set(gdn_a_log)
    dtb_row = tail.at[:, 0, SMALL_DA:SMALL_DB].set(gdn_dt_bias)

    pad_rows = 8
    gla_p, gla_s, gdn_p, gdn_s, conv_p, conv_s = [], [], [], [], [], []
    k_p, v_p, k_s, v_s = [], [], [], []
    for layer in range(DEPTH):
        i = layer // 2
        sh_p, sc_p, ga_p, shf_p, scf_p, gaf_p = _split_mods(mods[layer, :bp], None)
        sh_s, sc_s, ga_s, shf_s, scf_s, gaf_s = _split_mods(mods[layer, bp:bp + bs], dseq)
        if layer % 2 == 0:
            yp = mod_matmul(xp, sc_p, sh_p, w_lin[i], tm=512, tn=LIN_COLS // 5)
            ys = mod_matmul(xs, sc_s, sh_s, w_lin[i], tm=ts, tn=LIN_COLS // 5)
            yp3 = yp.reshape(bp, seq, LIN_COLS)
            ys3 = jnp.pad(ys.reshape(bs, dseq, LIN_COLS), ((0, 0), (0, pad_rows - dseq), (0, 0)))
            zeros_t = jnp.zeros((bp, DV, GLA_QK), F32)
            s0_t = state_gla[i].transpose(0, 3, 1, 2).reshape(bs, DV, GLA_QK)
            og_p, st_p = gla_heads(yp3, zeros_t, gla_wa2[i], gla_ba[i], gla_norm[i], rows=256, valid=256)
            og_s, st_s = gla_heads(ys3, s0_t, gla_wa2[i], gla_ba[i], gla_norm[i], rows=pad_rows, valid=dseq)
            gla_p.append(st_p.reshape(bp, DV, H_GLA, DK_GLA).transpose(0, 2, 3, 1))
            gla_s.append(st_s.reshape(bs, DV, H_GLA, DK_GLA).transpose(0, 2, 3, 1))
            cp_p = jnp.zeros((bp, 8, GDN_CONV_CH), F32)
            cp_s = jnp.pad(state_conv[i], ((0, 0), (8 - (CONV_W - 1), 0), (0, 0)))
            od_p, sd_p = gdn_heads(yp3, jnp.zeros((bp, H_GDN, DK_GDN, DV), F32), cp_p, gdn_conv[i],
                                   alog_row[i], dtb_row[i], gdn_norm[i], rows=256, valid=256)
            od_s, sd_s = gdn_heads(ys3, state_gdn[i], cp_s, gdn_conv[i],
                                   alog_row[i], dtb_row[i], gdn_norm[i], rows=pad_rows, valid=dseq)
            gdn_p.append(sd_p)
            gdn_s.append(sd_s)
            c0, c1 = 2 * GLA_QK + 2 * GLA_VW, 2 * GLA_QK + 2 * GLA_VW + GDN_CONV_CH
            conv_p.append(yp3[:, seq - (CONV_W - 1):, c0:c1])
            conv_s.append(jnp.concatenate([state_conv[i], ys.reshape(bs, dseq, LIN_COLS)[:, :, c0:c1]],
                                          axis=1)[:, dseq:])
            xp = proj_ln([og_p.reshape(tp, GLA_VW), od_p.reshape(tp, GDN_VW)],
                         [w_out[i, :GLA_VW], w_out[i, GLA_VW:]], xp, ga_p, ln_g[layer, 0], ln_b[layer, 0], tm=512)
            xs = proj_ln([og_s[:, :dseq].reshape(ts, GLA_VW), od_s[:, :dseq].reshape(ts, GDN_VW)],
                         [w_out[i, :GLA_VW], w_out[i, GLA_VW:]], xs, ga_s, ln_g[layer, 0], ln_b[layer, 0], tm=ts)
        else:
            yp = mod_matmul(xp, sc_p, sh_p, w_qkv[i], tm=512, tn=1024)
            ys = mod_matmul(xs, sc_s, sh_s, w_qkv[i], tm=ts, tn=1024)
            yp3 = yp.reshape(bp, seq, 3 * SB_W)
            ys3 = ys.reshape(bs, dseq, 3 * SB_W)
            bias = sb_bias[i]
            o_p = sb_prompt(yp3, jnp.broadcast_to(bias[:, None], (H_SB, 256)))
            q_s, kn_s, vn_s = (ys3[:, :, j * SB_W:(j + 1) * SB_W].reshape(bs, dseq, H_SB, DH_SB) for j in range(3))
            o_s = sb_sample(q_s, kn_s, vn_s, cache_k, cache_v, page_table, bias, i)
            k_p.append(yp3[:, :, SB_W:2 * SB_W].reshape(bp, seq, H_SB, DH_SB))
            v_p.append(yp3[:, :, 2 * SB_W:].reshape(bp, seq, H_SB, DH_SB))
            k_s.append(kn_s)
            v_s.append(vn_s)
            xp = proj_ln([o_p.reshape(tp, SB_W)], [w_o[i]], xp, ga_p, ln_g[layer, 0], ln_b[layer, 0], tm=512)
            xs = proj_ln([o_s.reshape(ts, SB_W)], [w_o[i]], xs, ga_s, ln_g[layer, 0], ln_b[layer, 0], tm=ts)
        mw = (w_gate, w_up, w_down, ws_gate, ws_up, ws_down, ln_g[layer, 1], ln_b[layer, 1], layer)
        gates_p = router(xp, scf_p, shf_p, w_r_t[layer], e_bias[layer], tm=512).T
        gates_s = router(xs, scf_s, shf_s, w_r_t[layer], e_bias[layer], tm=ts).T
        xp = moe_ln(xp, scf_p, shf_p, gaf_p, gates_p, *mw, tm=1024)
        xs = moe_ln(xs, scf_s, shf_s, gaf_s, gates_s, *mw, tm=ts)
    return (xp.reshape(bp, seq, d), xs.reshape(bs, dseq, d), jnp.stack(gla_p), jnp.stack(gla_s),
            jnp.stack(gdn_p), jnp.stack(gdn_s), jnp.stack(conv_p), jnp.stack(conv_s),
            jnp.stack(k_p), jnp.stack(v_p), jnp.stack(k_s), jnp.stack(v_s))
```

```python
import functools
import math

import jax
import jax.numpy as jnp
from jax import lax
from jax.experimental import pallas as pl
from jax.experimental.pallas import tpu as pltpu

F32 = jnp.float32
BF16 = jnp.bfloat16

D_MODEL = 1024
DEPTH = 4
PAGE_SIZE = 128
DV = 128
H_GLA = 4
DK_GLA = 64
GLA_RANK = 16
GLA_TAU = 16.0
H_GDN = 4
DK_GDN = 128
CONV_W = 4
H_SB = 16
DH_SB = 64
N_EXPERTS = 64
TOP_K = 8
N_GROUPS = 8
TOPK_GROUPS = 4
D_EXPERT = 256
ROUTED_SCALE = 2.5
DEEPNORM_ALPHA = (2.0 * DEPTH) ** 0.25
LN_EPS = 1e-5
RMS_EPS = 1e-6

GLA_QK = H_GLA * DK_GLA
GLA_VW = H_GLA * DV
GDN_QK = H_GDN * DK_GDN
GDN_VW = H_GDN * DV
GDN_CONV_CH = 2 * GDN_QK + GDN_VW
SB_W = H_SB * DH_SB
LIN_MAIN = 2 * GLA_QK + 2 * GLA_VW + 2 * GDN_QK + 2 * GDN_VW
LIN_COLS = LIN_MAIN + 256
SMALL_GA = 0
SMALL_DA = GLA_RANK
SMALL_DB = GLA_RANK + H_GDN

VMEM_LIMIT = 56 * 1024 * 1024
LOG2E = math.log2(math.e)
SB_UNROLL = 4
MASKED_LOGIT = -1e30


def _cparams(sem):
    return pltpu.CompilerParams(dimension_semantics=sem, vmem_limit_bytes=VMEM_LIMIT)


def _bdot(a, b):
    return jnp.dot(a.astype(BF16), b.astype(BF16), preferred_element_type=F32)


def _bdot_nt(a, b):
    return lax.dot_general(a.astype(BF16), b.astype(BF16), (((1,), (1,)), ((), ())),
                           preferred_element_type=F32)


def _bdot_tn(a, b):
    return lax.dot_general(a.astype(BF16), b.astype(BF16), (((0,), (0,)), ((), ())),
                           preferred_element_type=F32)


def _hilo(x):
    hi = x.astype(BF16)
    lo = (x - hi.astype(F32)).astype(BF16)
    return hi, lo


def _sel_dot_l(m01, x):
    hi, lo = _hilo(x)
    return (jnp.dot(m01, hi, preferred_element_type=F32)
            + jnp.dot(m01, lo, preferred_element_type=F32))


def _sel_dot_r(x, m01):
    hi, lo = _hilo(x)
    return (jnp.dot(hi, m01, preferred_element_type=F32)
            + jnp.dot(lo, m01, preferred_element_type=F32))


def _dot3(a, b, dims=(((1,), (0,)), ((), ()))):
    ah, al = _hilo(a)
    bh, bl = _hilo(b)
    dg = lambda x, y: lax.dot_general(x, y, dims, preferred_element_type=F32)
    return dg(ah, bh) + (dg(ah, bl) + dg(al, bh))


def _softplus(x):
    return jnp.maximum(x, 0.0) + jnp.log1p(jnp.exp(-jnp.abs(x)))


def _silu(x):
    return x * jax.nn.sigmoid(x)


def _layer_norm(r, g, b):
    mu = jnp.mean(r, axis=-1, keepdims=True)
    d = r - mu
    var = jnp.mean(d * d, axis=-1, keepdims=True)
    return d * lax.rsqrt(var + LN_EPS) * g + b


def _iota(shape, dim):
    return lax.broadcasted_iota(jnp.int32, shape, dim)


def _ada_kernel(c_ref, w_ref, b_ref, o_ref):
    a = _silu(c_ref[...]).astype(BF16)
    o_ref[0] = jnp.dot(a, w_ref[0].astype(BF16), preferred_element_type=F32) + b_ref[0]


def ada_all(c_all, w_ada, b_ada, tn=1536):
    nl, d, n = w_ada.shape
    bc = c_all.shape[0]
    return pl.pallas_call(
        _ada_kernel,
        grid=(nl, n // tn),
        in_specs=[pl.BlockSpec((bc, d), lambda l, j: (0, 0)),
                  pl.BlockSpec((1, d, tn), lambda l, j: (l, 0, j)),
                  pl.BlockSpec((1, 1, tn), lambda l, j: (l, 0, j))],
        out_specs=pl.BlockSpec((1, bc, tn), lambda l, j: (l, 0, j)),
        out_shape=jax.ShapeDtypeStruct((nl, bc, n), F32),
        compiler_params=_cparams(("parallel", "parallel")),
        name="ada_mod",
    )(c_all, w_ada, b_ada.reshape(nl, 1, n))


def _modmm_kernel(x_ref, sc_ref, sh_ref, w_ref, o_ref, h_ref):
    @pl.when(pl.program_id(1) == 0)
    def _():
        h_ref[...] = (x_ref[...] * (1.0 + sc_ref[0]) + sh_ref[0]).astype(BF16)

    o_ref[...] = jnp.dot(h_ref[...], w_ref[...], preferred_element_type=F32)


def mod_matmul(x, sc, sh, w, tm, tn):
    t, d = x.shape
    n = w.shape[1]
    nb, r, _ = sc.shape
    tpb = (t // tm) // nb
    mod_spec = pl.BlockSpec((1, r, d), lambda i, j: (i // tpb, 0, 0))
    return pl.pallas_call(
        _modmm_kernel,
        grid=(t // tm, n // tn),
        in_specs=[pl.BlockSpec((tm, d), lambda i, j: (i, 0)), mod_spec, mod_spec,
                  pl.BlockSpec((d, tn), lambda i, j: (0, j))],
        out_specs=pl.BlockSpec((tm, tn), lambda i, j: (i, j)),
        out_shape=jax.ShapeDtypeStruct((t, n), F32),
        scratch_shapes=[pltpu.VMEM((tm, d), BF16)],
        compiler_params=_cparams(("parallel", "arbitrary")),
        name="mod_matmul",
    )(x, sc, sh, w)


def _proj_ln_kernel(n_in, *refs):
    a_refs = refs[:n_in]
    w_refs = refs[n_in:2 * n_in]
    x_ref, g_ref, lg_ref, lb_ref, o_ref = refs[2 * n_in:]
    y = None
    for a_ref, w_ref in zip(a_refs, w_refs):
        p = jnp.dot(a_ref[...].astype(BF16), w_ref[...], preferred_element_type=F32)
        y = p if y is None else y + p
    r = DEEPNORM_ALPHA * x_ref[...] + (1.0 + g_ref[0]) * y
    o_ref[...] = _layer_norm(r, lg_ref[...], lb_ref[...])


def proj_ln(acts, ws, x, gate, ln_g, ln_b, tm):
    t, d = x.shape
    nb, r, _ = gate.shape
    tpb = (t // tm) // nb
    n_in = len(acts)
    in_specs = ([pl.BlockSpec((tm, a.shape[1]), lambda i: (i, 0)) for a in acts]
                + [pl.BlockSpec(w.shape, lambda i: (0, 0)) for w in ws]
                + [pl.BlockSpec((tm, d), lambda i: (i, 0)),
                   pl.BlockSpec((1, r, d), lambda i: (i // tpb, 0, 0)),
                   pl.BlockSpec((1, d), lambda i: (0, 0)),
                   pl.BlockSpec((1, d), lambda i: (0, 0))])
    return pl.pallas_call(
        functools.partial(_proj_ln_kernel, n_in),
        grid=(t // tm,),
        in_specs=in_specs,
        out_specs=pl.BlockSpec((tm, d), lambda i: (i, 0)),
        out_shape=jax.ShapeDtypeStruct((t, d), F32),
        compiler_params=_cparams(("parallel",)),
        name="proj_ln",
    )(*acts, *ws, x, gate, ln_g.reshape(1, d), ln_b.reshape(1, d))


def _gla_kernel(q_ref, k_ref, v_ref, r_ref, sm_ref, wa2_ref, ba_ref, gn_ref, s0_ref,
                o_ref, so_ref, st_ref, *, chunk, sub, rows, valid):
    j = pl.program_id(1)

    @pl.when(j == 0)
    def _():
        st_ref[...] = s0_ref[0]

    c = chunk
    live = min(c, rows)
    nch = max(rows // c, 1)
    nsub = c // sub
    tri = (_iota((c, c), 0) >= _iota((c, c), 1)).astype(BF16)
    lane_head = _iota((1, GLA_QK), 1) // DK_GLA

    def ext(x):
        if live == c:
            return x
        return jnp.concatenate([x, jnp.zeros((c - live, x.shape[1]), x.dtype)], axis=0)

    def stack_heads(x):
        return jnp.concatenate([jnp.where(lane_head == h, x, 0.0) for h in range(H_GLA)], axis=0)

    for ci in range(nch):
        sl = pl.ds(ci * c, live)
        q = ext(q_ref[0, sl, :]) * (DK_GLA ** -0.5)
        k = ext(k_ref[0, sl, :])
        v = ext(v_ref[0, sl, :])
        rg = ext(r_ref[0, sl, :])
        ga = ext(sm_ref[0, sl, :])[:, SMALL_GA:SMALL_GA + GLA_RANK]
        xg = _bdot(ga, wa2_ref[...]) + ba_ref[...]
        la = -_softplus(-xg) * (1.0 / GLA_TAU)
        if valid < c:
            la = jnp.where(_iota((c, GLA_QK), 0) < valid, la, 0.0)
        bc = _sel_dot_l(tri, la)
        st = st_ref[...]
        inter = _bdot_nt(stack_heads(q * jnp.exp(bc)), st)

        parts = [[None] * nsub for _ in range(H_GLA)]
        for si in range(nsub):
            lo = si * sub
            n = lo + sub
            r0 = bc[lo - 1:lo, :] if si > 0 else jnp.zeros((1, GLA_QK), F32)
            qi = q[lo:n] * jnp.exp(bc[lo:n] - r0)
            ki = k[:n] * jnp.exp(r0 - bc[:n])
            att = _bdot_nt(stack_heads(qi), ki)
            rr = _iota((H_GLA * sub, n), 0) & (sub - 1)
            cc = _iota((H_GLA * sub, n), 1) - lo
            att = jnp.where(cc <= rr, att, 0.0)
            ov = _bdot(att, v[:n])
            for h in range(H_GLA):
                parts[h][si] = ov[h * sub:(h + 1) * sub, h * DV:(h + 1) * DV]

        outs = []
        for h in range(H_GLA):
            intra = parts[h][0] if nsub == 1 else jnp.concatenate(parts[h], axis=0)
            o_h = inter[h * c:(h + 1) * c] + intra
            o_h = o_h * lax.rsqrt(jnp.mean(o_h * o_h, axis=-1, keepdims=True) + RMS_EPS) * gn_ref[...]
            outs.append(o_h * _silu(rg[:, h * DV:(h + 1) * DV]))
        o_full = jnp.concatenate(outs, axis=1)
        o_ref[0, sl, :] = o_full[:live]

        bl = bc[c - 1:c, :]
        kh = k * jnp.exp(bl - bc)
        new = st * jnp.exp(bl)
        for h in range(H_GLA):
            upd = _bdot_tn(v[:, h * DV:(h + 1) * DV], kh)
            new = new + jnp.where(lane_head == h, upd, 0.0)
        st_ref[...] = new

    @pl.when(j == pl.num_programs(1) - 1)
    def _():
        so_ref[0] = st_ref[...]


def gla_heads(y3, s0_t, wa2, ba, gn, *, rows, valid, chunk=64, sub=16):
    b, l, _ = y3.shape
    blk = lambda w, idx: pl.BlockSpec((1, rows, w), lambda bi, j: (bi, j, idx))
    const = lambda shp: pl.BlockSpec(shp, lambda bi, j: tuple(0 for _ in shp))
    return pl.pallas_call(
        functools.partial(_gla_kernel, chunk=chunk, sub=min(sub, chunk), rows=rows, valid=valid),
        grid=(b, l // rows),
        in_specs=[blk(GLA_QK, 0), blk(GLA_QK, 1), blk(GLA_VW, 1), blk(GLA_VW, 2),
                  blk(128, LIN_MAIN // 128),
                  const((GLA_RANK, GLA_QK)), const((1, GLA_QK)), const((1, DV)),
                  pl.BlockSpec((1, DV, GLA_QK), lambda bi, j: (bi, 0, 0))],
        out_specs=[pl.BlockSpec((1, rows, GLA_VW), lambda bi, j: (bi, j, 0)),
                   pl.BlockSpec((1, DV, GLA_QK), lambda bi, j: (bi, 0, 0))],
        out_shape=[jax.ShapeDtypeStruct((b, l, GLA_VW), F32),
                   jax.ShapeDtypeStruct((b, DV, GLA_QK), F32)],
        scratch_shapes=[pltpu.VMEM((DV, GLA_QK), F32)],
        compiler_params=_cparams(("parallel", "arbitrary")),
        name="gla_heads",
    )(y3, y3, y3, y3, y3, wa2, ba.reshape(1, GLA_QK), gn.reshape(1, DV), s0_t)


def _gdn_kernel(dq_ref, dk_ref, dv_ref, dz_ref, sm_ref, cw_ref, al_ref, dtb_ref, gn_ref, cp_ref, s0_ref,
                o_ref, so_ref, xx_ref, st_ref, *, chunk, rows, valid):
    j = pl.program_id(1)

    @pl.when(j == 0)
    def _():
        st_ref[...] = s0_ref[0]
        xx_ref[0:8, :] = cp_ref[0]

    c = chunk
    live = min(c, rows)
    nch = max(rows // c, 1)

    xx_ref[8:8 + rows, 0:GDN_QK] = dq_ref[0]
    xx_ref[8:8 + rows, GDN_QK:2 * GDN_QK] = dk_ref[0]
    xx_ref[8:8 + rows, 2 * GDN_QK:GDN_CONV_CH] = dv_ref[0]
    y = None
    for i in range(CONV_W):
        term = xx_ref[pl.ds(8 - (CONV_W - 1) + i, rows), :] * cw_ref[i:i + 1, :]
        y = term if y is None else y + term
    xx_ref[0:8, :] = xx_ref[rows:rows + 8, :]
    qkv = _silu(y)

    def ext(x):
        if live == c:
            return x
        return jnp.concatenate([x, jnp.zeros((c - live, x.shape[1]), x.dtype)], axis=0)

    ri = _iota((c, c), 0)
    cj = _iota((c, c), 1)
    tri = (ri >= cj).astype(BF16)
    incl = ri >= cj
    strict = ri > cj
    strict_f = strict.astype(F32)
    eye = (ri == cj).astype(F32)
    blk16 = (ri >> 4) == (cj >> 4)
    merge_masks = [((ri >> lv) == (cj >> lv)) & ((ri >> (lv - 1)) != (cj >> (lv - 1)))
                   for lv in range(5, int(math.log2(c)) + 1)]
    pick = (_iota((128, 2 * GDN_VW), 0) == SMALL_DA + (_iota((128, 2 * GDN_VW), 1) >> 7)).astype(BF16)
    lane = _iota((c, 128), 1)

    items = []
    for ci in range(nch):
        sl = pl.ds(ci * c, live)
        blk = ext(qkv[ci * c:ci * c + live])
        dz = ext(dz_ref[0, sl, :])
        sm = ext(sm_ref[0, sl, :])
        g_all = -jnp.exp(al_ref[...]) * _softplus(sm + dtb_ref[...])
        beta_all = jax.nn.sigmoid(sm)
        vals = jnp.where((lane >= SMALL_DA) & (lane < SMALL_DB), g_all,
                         jnp.where((lane >= SMALL_DB) & (lane < SMALL_DB + H_GDN), beta_all, 0.0))
        if valid < c:
            vals = jnp.where(_iota((c, 128), 0) < valid, vals, 0.0)
        gb = _sel_dot_r(vals, pick)
        g_b = gb[:, :GDN_VW]
        gam = _sel_dot_l(tri, g_b)
        egam = jnp.exp(gam)

        for h in range(H_GDN):
            hs = slice(h * DK_GDN, (h + 1) * DK_GDN)
            cq = blk[:, hs]
            ck = blk[:, GDN_QK + h * DK_GDN:GDN_QK + (h + 1) * DK_GDN]
            cv = blk[:, 2 * GDN_QK + h * DV:2 * GDN_QK + (h + 1) * DV]
            qn = cq * lax.rsqrt(jnp.sum(cq * cq, axis=-1, keepdims=True) + RMS_EPS) * (DK_GDN ** -0.5)
            kn = ck * lax.rsqrt(jnp.sum(ck * ck, axis=-1, keepdims=True) + RMS_EPS)
            beta = gb[:, GDN_VW + h * 128:GDN_VW + (h + 1) * 128]
            gm = gam[:, hs]
            eg = egam[:, hs]
            items.append(dict(ci=ci, h=h, sl=sl, qn=qn, kn=kn, beta=beta, gh=g_b[:, hs], gm=gm, eg=eg,
                              gl=gm[c - 1:c, :],
                              rhs=jnp.concatenate([cv * beta, kn * (beta * eg)], axis=1),
                              dz=dz[:, h * DV:(h + 1) * DV]))

    for it in items:
        it["dec"] = jnp.where(incl, jnp.exp(_sel_dot_l(tri, it["gh"] * strict_f)), 0.0)
        it["kk"] = _dot3(it["kn"], it["kn"], (((1,), (1,)), ((), ())))
    for it in items:
        it["nm"] = jnp.where(strict, it["kk"] * it["dec"] * it["beta"], 0.0)
        it["p"] = jnp.where(blk16, it["nm"], 0.0)
        it["t"] = eye - it["p"]
    for _ in range(3):
        for it in items:
            it["p"] = _dot3(it["p"], it["p"])
        for it in items:
            it["t"] = it["t"] + _dot3(it["t"], it["p"])
    for off in merge_masks:
        for it in items:
            it["tc"] = _dot3(it["t"], jnp.where(off, it["nm"], 0.0))
        for it in items:
            it["t"] = it["t"] - _dot3(it["tc"], it["t"])
    for it in items:
        it["x"] = _dot3(it["t"], it["rhs"])
        it["qk"] = jnp.where(incl, _bdot_nt(it["qn"], it["kn"]) * it["dec"], 0.0)

    state = [st_ref[h] for h in range(H_GDN)]
    for ci in range(nch):
        outs = []
        for it in items[ci * H_GDN:(ci + 1) * H_GDN]:
            h, s = it["h"], state[it["h"]]
            u = it["x"][:, :DV] - _bdot(it["x"][:, DV:], s)
            o_h = _bdot(it["qn"] * it["eg"], s) + _bdot(it["qk"], u)
            state[h] = s * jnp.exp(it["gl"]) + _bdot_tn(it["kn"] * jnp.exp(it["gl"] - it["gm"]), u)
            o_h = o_h * lax.rsqrt(jnp.mean(o_h * o_h, axis=-1, keepdims=True) + RMS_EPS) * gn_ref[...]
            outs.append(o_h * _silu(it["dz"]))
        o_full = jnp.concatenate(outs, axis=1)
        o_ref[0, items[ci * H_GDN]["sl"], :] = o_full[:live]
    for h in range(H_GDN):
        st_ref[h] = state[h]

    @pl.when(j == pl.num_programs(1) - 1)
    def _():
        so_ref[0] = st_ref[...]


def gdn_heads(y3, s0, conv_prev8, conv_w, alog_row, dtb_row, gn, *, rows, valid, chunk=128):
    b, l, _ = y3.shape
    blk = lambda idx: pl.BlockSpec((1, rows, 512), lambda bi, j: (bi, j, idx))
    const = lambda shp: pl.BlockSpec(shp, lambda bi, j: tuple(0 for _ in shp))
    return pl.pallas_call(
        functools.partial(_gdn_kernel, chunk=chunk, rows=rows, valid=valid),
        grid=(b, l // rows),
        in_specs=[blk(3), blk(4), blk(5), blk(6),
                  pl.BlockSpec((1, rows, 128), lambda bi, j: (bi, j, LIN_MAIN // 128)),
                  const((CONV_W, GDN_CONV_CH)), const((1, 128)), const((1, 128)), const((1, DV)),
                  pl.BlockSpec((1, 8, GDN_CONV_CH), lambda bi, j: (bi, 0, 0)),
                  pl.BlockSpec((1, H_GDN, DK_GDN, DV), lambda bi, j: (bi, 0, 0, 0))],
        out_specs=[pl.BlockSpec((1, rows, GDN_VW), lambda bi, j: (bi, j, 0)),
                   pl.BlockSpec((1, H_GDN, DK_GDN, DV), lambda bi, j: (bi, 0, 0, 0))],
        out_shape=[jax.ShapeDtypeStruct((b, l, GDN_VW), F32),
                   jax.ShapeDtypeStruct((b, H_GDN, DK_GDN, DV), F32)],
        scratch_shapes=[pltpu.VMEM((rows + 8, GDN_CONV_CH), F32),
                        pltpu.VMEM((H_GDN, DK_GDN, DV), F32)],
        compiler_params=_cparams(("parallel", "arbitrary")),
        name="gdn_heads",
    )(y3, y3, y3, y3, y3, conv_w, alog_row, dtb_row, gn.reshape(1, DV), conv_prev8, s0)


def _tri_ones():
    r = _iota((128, 256), 0)
    c = _iota((128, 256), 1)
    return ((r >= c) | (c >= 128)).astype(BF16)


def _sbp_kernel(q_ref, k_ref, v_ref, b_ref, o_ref, *, tq):
    hp = pl.program_id(1)
    i = pl.program_id(2)
    q = q_ref[0] * (DH_SB ** -0.5 * LOG2E)
    lane = _iota((1, 128), 1)
    to = _tri_ones()
    qm = [jnp.where((lane >= DH_SB) == (hh == 1), q, 0.0).astype(BF16) for hh in range(2)]
    bias = [b_ref[pl.ds(2 * hp + hh, 1), :] * LOG2E for hh in range(2)]
    diag_mask = _iota((tq, tq), 1) < _iota((tq, tq), 0)

    def rows_of(n):
        return pl.ds(pl.multiple_of(jnp.clip(i - n, 0, i) * tq, tq), tq)

    def logits(n, mask):
        kb = k_ref[0, rows_of(n), :].astype(BF16)
        off = jnp.where(n > i, MASKED_LOGIT, 0.0)
        zs = []
        for hh in range(2):
            z = lax.dot_general(qm[hh], kb, (((1,), (1,)), ((), ())), preferred_element_type=F32) + (bias[hh] + off)
            if mask is not None:
                z = jnp.where(mask, z, MASKED_LOGIT)
            zs.append(z)
        return tuple(zs)

    def suffix(zs):
        spbs = [(jnp.maximum(z, 0.0) + jnp.log2(1.0 + jnp.exp2(-jnp.abs(z)))).astype(BF16) for z in zs]
        r_hi = [jnp.dot(spb[:, 128:], to, preferred_element_type=F32) for spb in spbs]
        r_lo = [jnp.dot(spb[:, :128], to, preferred_element_type=F32) for spb in spbs]
        outs = []
        for z, hi, lo in zip(zs, r_hi, r_lo):
            totl = jnp.concatenate([lo[:, :128] + hi[:, 128:], hi[:, :128]], axis=1)
            outs.append((z - totl, lo[:, 128:] + hi[:, 128:]))
        return tuple(outs)

    def weigh(n, ds, state):
        carries, acc = state
        vb = v_ref[0, rows_of(n), :].astype(BF16)
        vcat = jnp.concatenate([jnp.where(lane < DH_SB, vb, 0), jnp.where(lane < DH_SB, 0, vb)], axis=0)
        a = [jnp.exp2(d - jnp.concatenate([c, c], axis=1)).astype(BF16) for (d, _), c in zip(ds, carries)]
        acc = acc + jnp.dot(jnp.concatenate(a, axis=1), vcat, preferred_element_type=F32)
        return tuple(c + rs for (_, rs), c in zip(ds, carries)), acc

    zero = jnp.where(lane < 0, q, 0.0)
    z0 = logits(0, diag_mask)
    idle = tuple((jnp.minimum(z, MASKED_LOGIT), zero) for z in z0)

    def step(t, carried):
        zs, ds, st = carried
        return logits(t + 1, None), suffix(zs), weigh(t - 1, ds, st)

    def body(tu, carried):
        for u in range(SB_UNROLL):
            carried = step(SB_UNROLL * tu + u, carried)
        return carried

    trips = lax.shift_right_logical(i + 2 + SB_UNROLL - 1, SB_UNROLL.bit_length() - 1)
    _, _, state = lax.fori_loop(0, trips, body, (z0, idle, ((zero, zero), zero)))
    o_ref[0] = state[1]


def sb_prompt(y3, bias_rows, tq=256):
    b, l, _ = y3.shape
    nhp = SB_W // 128
    return pl.pallas_call(
        functools.partial(_sbp_kernel, tq=tq),
        grid=(b, nhp, l // tq),
        in_specs=[pl.BlockSpec((1, tq, 128), lambda bi, hp, i: (bi, i, hp)),
                  pl.BlockSpec((1, l, 128), lambda bi, hp, i: (bi, 0, nhp + hp)),
                  pl.BlockSpec((1, l, 128), lambda bi, hp, i: (bi, 0, 2 * nhp + hp)),
                  pl.BlockSpec((H_SB, tq), lambda bi, hp, i: (0, 0))],
        out_specs=pl.BlockSpec((1, tq, 128), lambda bi, hp, i: (bi, i, hp)),
        out_shape=jax.ShapeDtypeStruct((b, l, SB_W), F32),
        compiler_params=_cparams(("parallel", "parallel", "arbitrary")),
        name="sb_prompt",
    )(y3, y3, y3, bias_rows)


def _sbs_kernel(pt_ref, q_ref, kn_ref, vn_ref, *rest, n_tok, group):
    k_refs = rest[:group]
    v_refs = rest[group:2 * group]
    b_ref, o_ref, acc_ref, carry_ref = rest[2 * group:]
    s = pl.program_id(1)
    rows = n_tok * H_SB
    own_head = (_iota((rows, SB_W), 1) >> 6) == (_iota((rows, SB_W), 0) & (H_SB - 1))
    qbd = jnp.where(own_head, q_ref[0] * (DH_SB ** -0.5 * LOG2E), 0.0).astype(BF16)
    causal_new = _iota((rows, PAGE_SIZE), 1) < (_iota((rows, PAGE_SIZE), 0) >> 4)
    to = _tri_ones()
    bias = b_ref[...]

    def sweep(kts, vts, keep):
        zs = [jnp.dot(qbd, kt.astype(BF16), preferred_element_type=F32) + bias for kt in kts]
        if keep is not None:
            zs = [jnp.where(keep, z, MASKED_LOGIT) for z in zs]
        sps = [(jnp.maximum(z, 0.0) + jnp.log2(1.0 + jnp.exp2(-jnp.abs(z)))).astype(BF16) for z in zs]
        rs = [jnp.dot(sp, to, preferred_element_type=F32) for sp in sps]
        carry = carry_ref[...]
        a = []
        for z, r in zip(zs, rs):
            a.append(jnp.exp2(z - (r[:, :128] + carry)).astype(BF16))
            carry = carry + r[:, 128:]
        carry_ref[...] = carry
        a_cat = a[0] if len(a) == 1 else jnp.concatenate(a, axis=1)
        vbs = [vt.astype(BF16) for vt in vts]
        v_cat = vbs[0] if len(vbs) == 1 else jnp.concatenate(vbs, axis=1)
        acc_ref[...] += lax.dot_general(a_cat, v_cat, (((1,), (1,)), ((), ())),
                                        preferred_element_type=F32)

    @pl.when(s == 0)
    def _():
        acc_ref[...] = jnp.zeros_like(acc_ref)
        carry_ref[...] = jnp.zeros_like(carry_ref)
        sweep([kn_ref[0]], [vn_ref[0]], causal_new)

    @pl.when(s > 0)
    def _():
        sweep([k_refs[g][0, 0] for g in range(group)], [v_refs[g][0, 0] for g in range(group)], None)

    @pl.when(s == pl.num_programs(1) - 1)
    def _():
        accm = jnp.where(own_head, acc_ref[...], 0.0)
        o_ref[0] = jnp.concatenate(
            [jnp.sum(accm[t * H_SB:(t + 1) * H_SB], axis=0, keepdims=True) for t in range(n_tok)], axis=0)


def sb_sample(q_s, k_new, v_new, cache_k, cache_v, page_table, bias, layer, group=4):
    b, n_tok, _, _ = q_s.shape
    nl, n_phys = cache_k.shape[:2]
    npages = page_table.shape[1]
    assert H_SB == 16 and npages % group == 0 and n_tok * H_SB <= 128
    rows = n_tok * H_SB
    ck = cache_k.transpose(0, 1, 3, 4, 2).reshape(nl, n_phys, SB_W, PAGE_SIZE)
    cv = cache_v.transpose(0, 1, 3, 4, 2).reshape(nl, n_phys, SB_W, PAGE_SIZE)
    pad = ((0, 0), (0, 0), (0, PAGE_SIZE - n_tok))
    kn = jnp.pad(k_new.reshape(b, n_tok, SB_W).transpose(0, 2, 1), pad)
    vn = jnp.pad(v_new.reshape(b, n_tok, SB_W).transpose(0, 2, 1), pad)
    qrep = jnp.repeat(q_s.reshape(b, n_tok, SB_W), H_SB, axis=1)
    bias_rows = jnp.broadcast_to((jnp.tile(bias, n_tok) * LOG2E)[:, None], (rows, PAGE_SIZE))

    def page_map(k):
        return lambda bi, s, pt: (layer, pt[bi, npages - 1 - (jnp.maximum(s, 1) - 1) * group - k], 0, 0)

    page_specs = [pl.BlockSpec((1, 1, SB_W, PAGE_SIZE), page_map(k)) for k in range(group)]
    grid_spec = pltpu.PrefetchScalarGridSpec(
        num_scalar_prefetch=1,
        grid=(b, npages // group + 1),
        in_specs=[pl.BlockSpec((1, rows, SB_W), lambda bi, s, pt: (bi, 0, 0)),
                  pl.BlockSpec((1, SB_W, PAGE_SIZE), lambda bi, s, pt: (bi, 0, 0)),
                  pl.BlockSpec((1, SB_W, PAGE_SIZE), lambda bi, s, pt: (bi, 0, 0))]
                 + page_specs + page_specs
                 + [pl.BlockSpec((rows, PAGE_SIZE), lambda bi, s, pt: (0, 0))],
        out_specs=pl.BlockSpec((1, n_tok, SB_W), lambda bi, s, pt: (bi, 0, 0)),
        scratch_shapes=[pltpu.VMEM((rows, SB_W), F32), pltpu.VMEM((rows, PAGE_SIZE), F32)],
    )
    out = pl.pallas_call(
        functools.partial(_sbs_kernel, n_tok=n_tok, group=group),
        grid_spec=grid_spec,
        out_shape=jax.ShapeDtypeStruct((b, n_tok, SB_W), F32),
        compiler_params=_cparams(("parallel", "arbitrary")),
        name="sb_sample",
    )(page_table, qrep, kn, vn, *([ck] * group), *([cv] * group), bias_rows)
    return out.reshape(b, n_tok, H_SB, DH_SB)


def _router_kernel(x_ref, sc_ref, sh_ref, wrt_ref, eb_ref, o_ref):
    h = x_ref[...] * (1.0 + sc_ref[0]) + sh_ref[0]
    tm = h.shape[0]
    hh, hl = _hilo(h)
    wh, wl = _hilo(wrt_ref[...])
    nt = lambda a, b: lax.dot_general(a, b, (((1,), (1,)), ((), ())), preferred_element_type=F32)
    logits = nt(wh, hh) + nt(wh, hl) + nt(wl, hh)
    s = jax.nn.sigmoid(logits)
    sel = s + eb_ref[...]
    gsz = N_EXPERTS // N_GROUPS
    ninf = -jnp.inf
    g3 = sel.reshape(N_GROUPS, gsz, tm)
    io3 = _iota((N_GROUPS, gsz, tm), 1).astype(F32)
    m1 = jnp.max(g3, axis=1, keepdims=True)
    i1 = jnp.min(jnp.where(g3 == m1, io3, float(gsz)), axis=1, keepdims=True)
    m2 = jnp.max(jnp.where(io3 == i1, ninf, g3), axis=1, keepdims=True)
    gscore = (m1 + m2).reshape(N_GROUPS, tm)
    iog = _iota((N_GROUPS, tm), 0).astype(F32)
    gsel = jnp.zeros((N_GROUPS, tm), F32)
    cur = gscore
    for _ in range(TOPK_GROUPS):
        m = jnp.max(cur, axis=0, keepdims=True)
        idx = jnp.min(jnp.where(cur == m, iog, float(N_GROUPS)), axis=0, keepdims=True)
        hit = iog == idx
        gsel = jnp.where(hit, 1.0, gsel)
        cur = jnp.where(hit, ninf, cur)
    emask = jnp.broadcast_to(gsel.reshape(N_GROUPS, 1, tm), (N_GROUPS, gsz, tm)).reshape(N_EXPERTS, tm)
    cur = jnp.where(emask > 0.0, sel, ninf)
    ioe = _iota((N_EXPERTS, tm), 0).astype(F32)
    chosen = jnp.zeros((N_EXPERTS, tm), F32)
    for _ in range(TOP_K):
        m = jnp.max(cur, axis=0, keepdims=True)
        idx = jnp.min(jnp.where(cur == m, ioe, float(N_EXPERTS)), axis=0, keepdims=True)
        hit = ioe == idx
        chosen = jnp.where(hit, 1.0, chosen)
        cur = jnp.where(hit, ninf, cur)
    wsel = jnp.where(chosen > 0.0, s, 0.0)
    o_ref[...] = wsel / jnp.sum(wsel, axis=0, keepdims=True) * ROUTED_SCALE


def router(x, sc, sh, w_r_t, e_b, tm):
    t, d = x.shape
    nb, r, _ = sc.shape
    tpb = (t // tm) // nb
    mod_spec = pl.BlockSpec((1, r, d), lambda i: (i // tpb, 0, 0))
    return pl.pallas_call(
        _router_kernel,
        grid=(t // tm,),
        in_specs=[pl.BlockSpec((tm, d), lambda i: (i, 0)), mod_spec, mod_spec,
                  pl.BlockSpec((N_EXPERTS, d), lambda i: (0, 0)),
                  pl.BlockSpec((N_EXPERTS, 1), lambda i: (0, 0))],
        out_specs=pl.BlockSpec((N_EXPERTS, tm), lambda i: (0, i)),
        out_shape=jax.ShapeDtypeStruct((N_EXPERTS, t), F32),
        compiler_params=_cparams(("parallel",)),
        name="moe_router",
    )(x, sc, sh, w_r_t, e_b.reshape(N_EXPERTS, 1))


def _moe_kernel(x_ref, sc_ref, sh_ref, ga_ref, gates_ref, wg_ref, wu_ref, wd_ref, sg_ref, su_ref, sd_ref,
                lg_ref, lb_ref, o_ref, hb_ref, acc_ref, g2_ref, *, eps):
    e = pl.program_id(1)
    n_routed_steps = N_EXPERTS // eps

    @pl.when(e == 0)
    def _():
        hb_ref[...] = (x_ref[...] * (1.0 + sc_ref[0]) + sh_ref[0]).astype(BF16)
        acc_ref[...] = jnp.zeros_like(acc_ref)
        g = gates_ref[...]
        gh = g.astype(BF16)
        g2_ref[:, :N_EXPERTS] = gh
        g2_ref[:, N_EXPERTS:] = (g - gh.astype(F32)).astype(BF16)

    def experts(wgs, wus, wd_rows, gcols):
        f = wgs[0].shape[1]
        w_in = [w.astype(BF16) for pair in zip(wgs, wus) for w in pair]
        gu = jnp.dot(hb_ref[...], w_in[0] if len(w_in) == 1 else jnp.concatenate(w_in, axis=1),
                     preferred_element_type=F32)
        acts = []
        for k, gcol in enumerate(gcols):
            a = _silu(gu[:, 2 * k * f:(2 * k + 1) * f]) * gu[:, (2 * k + 1) * f:(2 * k + 2) * f]
            acts.append((a if gcol is None else a * gcol).astype(BF16))
        a_cat = acts[0] if len(acts) == 1 else jnp.concatenate(acts, axis=1)
        acc_ref[...] += jnp.dot(a_cat, wd_rows.astype(BF16), preferred_element_type=F32)

    @pl.when(e < n_routed_steps)
    def _():
        gcols = []
        for k in range(eps):
            pick = ((_iota((2 * N_EXPERTS, D_EXPERT), 0) & (N_EXPERTS - 1)) == e * eps + k).astype(BF16)
            gcols.append(jnp.dot(g2_ref[...], pick, preferred_element_type=F32))
        wd = wd_ref[...]
        experts([wg_ref[k] for k in range(eps)], [wu_ref[k] for k in range(eps)],
                wd.reshape(wd.shape[0] * wd.shape[1], wd.shape[2]), gcols)

    @pl.when(e == n_routed_steps)
    def _():
        experts([sg_ref[...]], [su_ref[...]], sd_ref[...], [None])
        r = DEEPNORM_ALPHA * x_ref[...] + (1.0 + ga_ref[0]) * acc_ref[...]
        o_ref[...] = _layer_norm(r, lg_ref[...], lb_ref[...])


def moe_ln(x, sc, sh, ga, gates, wg, wu, wd, sg, su, sd, ln_g, ln_b, layer, tm, eps=2):
    t, d = x.shape
    nb, r, _ = sc.shape
    tpb = (t // tm) // nb
    f = wg.shape[3]
    last = N_EXPERTS // eps - 1
    mod_spec = pl.BlockSpec((1, r, d), lambda i, e: (i // tpb, 0, 0))
    const2 = lambda shp: pl.BlockSpec(shp, lambda i, e: (0, 0))
    routed = lambda a, b: pl.BlockSpec((None, eps, a, b), lambda i, e: (layer, jnp.minimum(e, last), 0, 0))
    shared = lambda a, b: pl.BlockSpec((None, a, b), lambda i, e: (layer, 0, 0))
    return pl.pallas_call(
        functools.partial(_moe_kernel, eps=eps),
        grid=(t // tm, N_EXPERTS // eps + 1),
        in_specs=[pl.BlockSpec((tm, d), lambda i, e: (i, 0)), mod_spec, mod_spec, mod_spec,
                  pl.BlockSpec((tm, N_EXPERTS), lambda i, e: (i, 0)),
                  routed(d, f), routed(d, f), routed(f, d),
                  shared(d, f), shared(d, f), shared(f, d), const2((1, d)), const2((1, d))],
        out_specs=pl.BlockSpec((tm, d), lambda i, e: (i, 0)),
        out_shape=jax.ShapeDtypeStruct((t, d), F32),
        scratch_shapes=[pltpu.VMEM((tm, d), BF16), pltpu.VMEM((tm, d), F32),
                        pltpu.VMEM((tm, 2 * N_EXPERTS), BF16)],
        compiler_params=_cparams(("parallel", "arbitrary")),
        name="moe_experts_ln",
    )(x, sc, sh, ga, gates, wg, wu, wd, sg, su, sd, ln_g.reshape(1, d), ln_b.reshape(1, d))


def _split_mods(m, per_row_repeat):
    parts = jnp.split(m, 6, axis=-1)
    if per_row_repeat is None:
        return [p[:, None, :] for p in parts]
    return [jnp.repeat(p, per_row_repeat, axis=0)[None] for p in parts]


def kernel(x_prompt, x_sample, state_gla, state_gdn, state_conv, cache_k, cache_v, page_table, c_prompt, c_sample, w_ada, b_ada, ln_g, ln_b, w_in_lin, gla_wa2, gla_ba, gla_norm, gdn_conv, gdn_a_log, gdn_dt_bias, gdn_norm, w_out_lin, w_qkv_sb, w_o_sb, sb_bias, w_router, e_bias, w_gate, w_up, w_down, ws_gate, ws_up, ws_down):
    bp, seq, d = x_prompt.shape
    bs, dseq, _ = x_sample.shape
    tp = bp * seq
    ts = bs * dseq
    n_phys = cache_k.shape[1]

    xp = x_prompt.reshape(tp, d)
    xs = x_sample.reshape(ts, d)

    mods = ada_all(jnp.concatenate([c_prompt, c_sample], axis=0), w_ada, b_ada)

    w_lin = jnp.concatenate(
        [w_in_lin[:, :, :1536], w_in_lin[:, :, 1552:3600], w_in_lin[:, :, 1536:1552], w_in_lin[:, :, 3600:3608],
         jnp.zeros((w_in_lin.shape[0], d, LIN_COLS - LIN_MAIN - GLA_RANK - 2 * H_GDN), w_in_lin.dtype)],
        axis=-1).astype(BF16)
    w_out = w_out_lin.astype(BF16)
    w_qkv = w_qkv_sb.astype(BF16)
    w_o = w_o_sb.astype(BF16)
    w_r_t = jnp.swapaxes(w_router, 1, 2)
    tail = jnp.zeros((w_in_lin.shape[0], 1, 128), F32)
    alog_row = tail.at[:, 0, SMALL_DA:SMALL_DB].set(gdn_a_log)
    dtb_row = tail.at[:, 0, SMALL_DA:SMALL_DB].set(gdn_dt_bias)

    pad_rows = 8
    gla_p, gla_s, gdn_p, gdn_s, conv_p, conv_s = [], [], [], [], [], []
    k_p, v_p, k_s, v_s = [], [], [], []
    for layer in range(DEPTH):
        i = layer // 2
        sh_p, sc_p, ga_p, shf_p, scf_p, gaf_p = _split_mods(mods[layer, :bp], None)
        sh_s, sc_s, ga_s, shf_s, scf_s, gaf_s = _split_mods(mods[layer, bp:bp + bs], dseq)
        if layer % 2 == 0:
            yp = mod_matmul(xp, sc_p, sh_p, w_lin[i], tm=1024, tn=LIN_COLS // 3)
            ys = mod_matmul(xs, sc_s, sh_s, w_lin[i], tm=ts, tn=LIN_COLS // 5)
            yp3 = yp.reshape(bp, seq, LIN_COLS)
            ys3 = jnp.pad(ys.reshape(bs, dseq, LIN_COLS), ((0, 0), (0, pad_rows - dseq), (0, 0)))
            zeros_t = jnp.zeros((bp, DV, GLA_QK), F32)
            s0_t = state_gla[i].transpose(0, 3, 1, 2).reshape(bs, DV, GLA_QK)
            og_p, st_p = gla_heads(yp3, zeros_t, gla_wa2[i], gla_ba[i], gla_norm[i], rows=256, valid=256)
            og_s, st_s = gla_heads(ys3, s0_t, gla_wa2[i], gla_ba[i], gla_norm[i], rows=pad_rows, valid=dseq)
            gla_p.append(st_p.reshape(bp, DV, H_GLA, DK_GLA).transpose(0, 2, 3, 1))
            gla_s.append(st_s.reshape(bs, DV, H_GLA, DK_GLA).transpose(0, 2, 3, 1))
            cp_p = jnp.zeros((bp, 8, GDN_CONV_CH), F32)
            cp_s = jnp.pad(state_conv[i], ((0, 0), (8 - (CONV_W - 1), 0), (0, 0)))
            od_p, sd_p = gdn_heads(yp3, jnp.zeros((bp, H_GDN, DK_GDN, DV), F32), cp_p, gdn_conv[i],
                                   alog_row[i], dtb_row[i], gdn_norm[i], rows=256, valid=256)
            od_s, sd_s = gdn_heads(ys3, state_gdn[i], cp_s, gdn_conv[i],
                                   alog_row[i], dtb_row[i], gdn_norm[i], rows=pad_rows, valid=dseq)
            gdn_p.append(sd_p)
            gdn_s.append(sd_s)
            c0, c1 = 2 * GLA_QK + 2 * GLA_VW, 2 * GLA_QK + 2 * GLA_VW + GDN_CONV_CH
            conv_p.append(yp3[:, seq - (CONV_W - 1):, c0:c1])
            conv_s.append(jnp.concatenate([state_conv[i], ys.reshape(bs, dseq, LIN_COLS)[:, :, c0:c1]],
                                          axis=1)[:, dseq:])
            xp = proj_ln([og_p.reshape(tp, GLA_VW), od_p.reshape(tp, GDN_VW)],
                         [w_out[i, :GLA_VW], w_out[i, GLA_VW:]], xp, ga_p, ln_g[layer, 0], ln_b[layer, 0], tm=512)
            xs = proj_ln([og_s[:, :dseq].reshape(ts, GLA_VW), od_s[:, :dseq].reshape(ts, GDN_VW)],
                         [w_out[i, :GLA_VW], w_out[i, GLA_VW:]], xs, ga_s, ln_g[layer, 0], ln_b[layer, 0], tm=ts)
        else:
            yp = mod_matmul(xp, sc_p, sh_p, w_qkv[i], tm=1024, tn=1536)
            ys = mod_matmul(xs, sc_s, sh_s, w_qkv[i], tm=ts, tn=1024)
            yp3 = yp.reshape(bp, seq, 3 * SB_W)
            ys3 = ys.reshape(bs, dseq, 3 * SB_W)
            bias = sb_bias[i]
            o_p = sb_prompt(yp3, jnp.broadcast_to(bias[:, None], (H_SB, 256)))
            q_s, kn_s, vn_s = (ys3[:, :, j * SB_W:(j + 1) * SB_W].reshape(bs, dseq, H_SB, DH_SB) for j in range(3))
            o_s = sb_sample(q_s, kn_s, vn_s, cache_k, cache_v, page_table, bias, i)
            k_p.append(yp3[:, :, SB_W:2 * SB_W].reshape(bp, seq, H_SB, DH_SB))
            v_p.append(yp3[:, :, 2 * SB_W:].reshape(bp, seq, H_SB, DH_SB))
            k_s.append(kn_s)
            v_s.append(vn_s)
            xp = proj_ln([o_p.reshape(tp, SB_W)], [w_o[i]], xp, ga_p, ln_g[layer, 0], ln_b[layer, 0], tm=512)
            xs = proj_ln([o_s.reshape(ts, SB_W)], [w_o[i]], xs, ga_s, ln_g[layer, 0], ln_b[layer, 0], tm=ts)
        mw = (w_gate, w_up, w_down, ws_gate, ws_up, ws_down, ln_g[layer, 1], ln_b[layer, 1], layer)
        gates_p = router(xp, scf_p, shf_p, w_r_t[layer], e_bias[layer], tm=512).T
        gates_s = router(xs, scf_s, shf_s, w_r_t[layer], e_bias[layer], tm=ts).T
        xp = moe_ln(xp, scf_p, shf_p, gaf_p, gates_p, *mw, tm=1024)
        xs = moe_ln(xs, scf_s, shf_s, gaf_s, gates_s, *mw, tm=ts)
    return (xp.reshape(bp, seq, d), xs.reshape(bs, dseq, d), jnp.stack(gla_p), jnp.stack(gla_s),
            jnp.stack(gdn_p), jnp.stack(gdn_s), jnp.stack(conv_p), jnp.stack(conv_s),
            jnp.stack(k_p), jnp.stack(v_p), jnp.stack(k_s), jnp.stack(v_s))
```

```python
import functools
import math

import jax
import jax.numpy as jnp
from jax import lax
from jax.experimental import pallas as pl
from jax.experimental.pallas import tpu as pltpu

F32 = jnp.float32
BF16 = jnp.bfloat16

D_MODEL = 1024
DEPTH = 4
PAGE_SIZE = 128
DV = 128
H_GLA = 4
DK_GLA = 64
GLA_RANK = 16
GLA_TAU = 16.0
H_GDN = 4
DK_GDN = 128
CONV_W = 4
H_SB = 16
DH_SB = 64
N_EXPERTS = 64
TOP_K = 8
N_GROUPS = 8
TOPK_GROUPS = 4
D_EXPERT = 256
ROUTED_SCALE = 2.5
DEEPNORM_ALPHA = (2.0 * DEPTH) ** 0.25
LN_EPS = 1e-5
RMS_EPS = 1e-6

GLA_QK = H_GLA * DK_GLA
GLA_VW = H_GLA * DV
GDN_QK = H_GDN * DK_GDN
GDN_VW = H_GDN * DV
GDN_CONV_CH = 2 * GDN_QK + GDN_VW
SB_W = H_SB * DH_SB
LIN_MAIN = 2 * GLA_QK + 2 * GLA_VW + 2 * GDN_QK + 2 * GDN_VW
LIN_COLS = LIN_MAIN + 256
SMALL_GA = 0
SMALL_DA = GLA_RANK
SMALL_DB = GLA_RANK + H_GDN

VMEM_LIMIT = 56 * 1024 * 1024
LOG2E = math.log2(math.e)
SB_UNROLL = 4
MASKED_LOGIT = -1e30


def _cparams(sem):
    return pltpu.CompilerParams(dimension_semantics=sem, vmem_limit_bytes=VMEM_LIMIT)


def _bdot(a, b):
    return jnp.dot(a.astype(BF16), b.astype(BF16), preferred_element_type=F32)


def _bdot_nt(a, b):
    return lax.dot_general(a.astype(BF16), b.astype(BF16), (((1,), (1,)), ((), ())),
                           preferred_element_type=F32)


def _bdot_tn(a, b):
    return lax.dot_general(a.astype(BF16), b.astype(BF16), (((0,), (0,)), ((), ())),
                           preferred_element_type=F32)


def _hilo(x):
    hi = x.astype(BF16)
    lo = (x - hi.astype(F32)).astype(BF16)
    return hi, lo


def _sel_dot_l(m01, x):
    hi, lo = _hilo(x)
    return (jnp.dot(m01, hi, preferred_element_type=F32)
            + jnp.dot(m01, lo, preferred_element_type=F32))


def _sel_dot_r(x, m01):
    hi, lo = _hilo(x)
    return (jnp.dot(hi, m01, preferred_element_type=F32)
            + jnp.dot(lo, m01, preferred_element_type=F32))


def _dot3(a, b, dims=(((1,), (0,)), ((), ()))):
    ah, al = _hilo(a)
    bh, bl = _hilo(b)
    dg = lambda x, y: lax.dot_general(x, y, dims, preferred_element_type=F32)
    return dg(ah, bh) + (dg(ah, bl) + dg(al, bh))


def _softplus(x):
    return jnp.maximum(x, 0.0) + jnp.log1p(jnp.exp(-jnp.abs(x)))


def _silu(x):
    return x * jax.nn.sigmoid(x)


def _layer_norm(r, g, b):
    mu = jnp.mean(r, axis=-1, keepdims=True)
    d = r - mu
    var = jnp.mean(d * d, axis=-1, keepdims=True)
    return d * lax.rsqrt(var + LN_EPS) * g + b


def _iota(shape, dim):
    return lax.broadcasted_iota(jnp.int32, shape, dim)


def _ada_kernel(c_ref, w_ref, b_ref, o_ref):
    a = _silu(c_ref[...]).astype(BF16)
    o_ref[0] = jnp.dot(a, w_ref[0].astype(BF16), preferred_element_type=F32) + b_ref[0]


def ada_all(c_all, w_ada, b_ada, tn=1536):
    nl, d, n = w_ada.shape
    bc = c_all.shape[0]
    return pl.pallas_call(
        _ada_kernel,
        grid=(nl, n // tn),
        in_specs=[pl.BlockSpec((bc, d), lambda l, j: (0, 0)),
                  pl.BlockSpec((1, d, tn), lambda l, j: (l, 0, j)),
                  pl.BlockSpec((1, 1, tn), lambda l, j: (l, 0, j))],
        out_specs=pl.BlockSpec((1, bc, tn), lambda l, j: (l, 0, j)),
        out_shape=jax.ShapeDtypeStruct((nl, bc, n), F32),
        compiler_params=_cparams(("parallel", "parallel")),
        name="ada_mod",
    )(c_all, w_ada, b_ada.reshape(nl, 1, n))


def _modmm_kernel(x_ref, sc_ref, sh_ref, w_ref, o_ref, h_ref):
    @pl.when(pl.program_id(1) == 0)
    def _():
        h_ref[...] = (x_ref[...] * (1.0 + sc_ref[0]) + sh_ref[0]).astype(BF16)

    o_ref[...] = jnp.dot(h_ref[...], w_ref[...], preferred_element_type=F32)


def mod_matmul(x, sc, sh, w, tm, tn):
    t, d = x.shape
    n = w.shape[1]
    nb, r, _ = sc.shape
    tpb = (t // tm) // nb
    mod_spec = pl.BlockSpec((1, r, d), lambda i, j: (i // tpb, 0, 0))
    return pl.pallas_call(
        _modmm_kernel,
        grid=(t // tm, n // tn),
        in_specs=[pl.BlockSpec((tm, d), lambda i, j: (i, 0)), mod_spec, mod_spec,
                  pl.BlockSpec((d, tn), lambda i, j: (0, j))],
        out_specs=pl.BlockSpec((tm, tn), lambda i, j: (i, j)),
        out_shape=jax.ShapeDtypeStruct((t, n), F32),
        scratch_shapes=[pltpu.VMEM((tm, d), BF16)],
        compiler_params=_cparams(("parallel", "arbitrary")),
        name="mod_matmul",
    )(x, sc, sh, w)


def _proj_ln_kernel(n_in, *refs):
    a_refs = refs[:n_in]
    w_refs = refs[n_in:2 * n_in]
    x_ref, g_ref, lg_ref, lb_ref, o_ref = refs[2 * n_in:]
    y = None
    for a_ref, w_ref in zip(a_refs, w_refs):
        p = jnp.dot(a_ref[...].astype(BF16), w_ref[...], preferred_element_type=F32)
        y = p if y is None else y + p
    r = DEEPNORM_ALPHA * x_ref[...] + (1.0 + g_ref[0]) * y
    o_ref[...] = _layer_norm(r, lg_ref[...], lb_ref[...])


def proj_ln(acts, ws, x, gate, ln_g, ln_b, tm):
    t, d = x.shape
    nb, r, _ = gate.shape
    tpb = (t // tm) // nb
    n_in = len(acts)
    in_specs = ([pl.BlockSpec((tm, a.shape[1]), lambda i: (i, 0)) for a in acts]
                + [pl.BlockSpec(w.shape, lambda i: (0, 0)) for w in ws]
                + [pl.BlockSpec((tm, d), lambda i: (i, 0)),
                   pl.BlockSpec((1, r, d), lambda i: (i // tpb, 0, 0)),
                   pl.BlockSpec((1, d), lambda i: (0, 0)),
                   pl.BlockSpec((1, d), lambda i: (0, 0))])
    return pl.pallas_call(
        functools.partial(_proj_ln_kernel, n_in),
        grid=(t // tm,),
        in_specs=in_specs,
        out_specs=pl.BlockSpec((tm, d), lambda i: (i, 0)),
        out_shape=jax.ShapeDtypeStruct((t, d), F32),
        compiler_params=_cparams(("parallel",)),
        name="proj_ln",
    )(*acts, *ws, x, gate, ln_g.reshape(1, d), ln_b.reshape(1, d))


def _gla_kernel(q_ref, k_ref, v_ref, r_ref, sm_ref, wa2_ref, ba_ref, gn_ref, s0_ref,
                o_ref, so_ref, st_ref, *, chunk, sub, rows, valid):
    j = pl.program_id(1)

    @pl.when(j == 0)
    def _():
        st_ref[...] = s0_ref[0]

    c = chunk
    live = min(c, rows)
    nch = max(rows // c, 1)
    nsub = c // sub
    tri = (_iota((c, c), 0) >= _iota((c, c), 1)).astype(BF16)
    lane_head = _iota((1, GLA_QK), 1) // DK_GLA

    def ext(x):
        if live == c:
            return x
        return jnp.concatenate([x, jnp.zeros((c - live, x.shape[1]), x.dtype)], axis=0)

    def stack_heads(x):
        return jnp.concatenate([jnp.where(lane_head == h, x, 0.0) for h in range(H_GLA)], axis=0)

    for ci in range(nch):
        sl = pl.ds(ci * c, live)
        q = ext(q_ref[0, sl, :]) * (DK_GLA ** -0.5)
        k = ext(k_ref[0, sl, :])
        v = ext(v_ref[0, sl, :])
        rg = ext(r_ref[0, sl, :])
        ga = ext(sm_ref[0, sl, :])[:, SMALL_GA:SMALL_GA + GLA_RANK]
        xg = _bdot(ga, wa2_ref[...]) + ba_ref[...]
        la = -_softplus(-xg) * (1.0 / GLA_TAU)
        if valid < c:
            la = jnp.where(_iota((c, GLA_QK), 0) < valid, la, 0.0)
        bc = _sel_dot_l(tri, la)
        st = st_ref[...]
        inter = _bdot_nt(stack_heads(q * jnp.exp(bc)), st)

        parts = [[None] * nsub for _ in range(H_GLA)]
        for si in range(nsub):
            lo = si * sub
            n = lo + sub
            r0 = bc[lo - 1:lo, :] if si > 0 else jnp.zeros((1, GLA_QK), F32)
            qi = q[lo:n] * jnp.exp(bc[lo:n] - r0)
            ki = k[:n] * jnp.exp(r0 - bc[:n])
            att = _bdot_nt(stack_heads(qi), ki)
            rr = _iota((H_GLA * sub, n), 0) & (sub - 1)
            cc = _iota((H_GLA * sub, n), 1) - lo
            att = jnp.where(cc <= rr, att, 0.0)
            ov = _bdot(att, v[:n])
            for h in range(H_GLA):
                parts[h][si] = ov[h * sub:(h + 1) * sub, h * DV:(h + 1) * DV]

        outs = []
        for h in range(H_GLA):
            intra = parts[h][0] if nsub == 1 else jnp.concatenate(parts[h], axis=0)
            o_h = inter[h * c:(h + 1) * c] + intra
            o_h = o_h * lax.rsqrt(jnp.mean(o_h * o_h, axis=-1, keepdims=True) + RMS_EPS) * gn_ref[...]
            outs.append(o_h * _silu(rg[:, h * DV:(h + 1) * DV]))
        o_full = jnp.concatenate(outs, axis=1)
        o_ref[0, sl, :] = o_full[:live]

        bl = bc[c - 1:c, :]
        kh = k * jnp.exp(bl - bc)
        new = st * jnp.exp(bl)
        for h in range(H_GLA):
            upd = _bdot_tn(v[:, h * DV:(h + 1) * DV], kh)
            new = new + jnp.where(lane_head == h, upd, 0.0)
        st_ref[...] = new

    @pl.when(j == pl.num_programs(1) - 1)
    def _():
        so_ref[0] = st_ref[...]


def gla_heads(y3, s0_t, wa2, ba, gn, *, rows, valid, chunk=64, sub=16):
    b, l, _ = y3.shape
    blk = lambda w, idx: pl.BlockSpec((1, rows, w), lambda bi, j: (bi, j, idx))
    const = lambda shp: pl.BlockSpec(shp, lambda bi, j: tuple(0 for _ in shp))
    return pl.pallas_call(
        functools.partial(_gla_kernel, chunk=chunk, sub=min(sub, chunk), rows=rows, valid=valid),
        grid=(b, l // rows),
        in_specs=[blk(GLA_QK, 0), blk(GLA_QK, 1), blk(GLA_VW, 1), blk(GLA_VW, 2),
                  blk(128, LIN_MAIN // 128),
                  const((GLA_RANK, GLA_QK)), const((1, GLA_QK)), const((1, DV)),
                  pl.BlockSpec((1, DV, GLA_QK), lambda bi, j: (bi, 0, 0))],
        out_specs=[pl.BlockSpec((1, rows, GLA_VW), lambda bi, j: (bi, j, 0)),
                   pl.BlockSpec((1, DV, GLA_QK), lambda bi, j: (bi, 0, 0))],
        out_shape=[jax.ShapeDtypeStruct((b, l, GLA_VW), F32),
                   jax.ShapeDtypeStruct((b, DV, GLA_QK), F32)],
        scratch_shapes=[pltpu.VMEM((DV, GLA_QK), F32)],
        compiler_params=_cparams(("parallel", "arbitrary")),
        name="gla_heads",
    )(y3, y3, y3, y3, y3, wa2, ba.reshape(1, GLA_QK), gn.reshape(1, DV), s0_t)


def _gdn_kernel(dq_ref, dk_ref, dv_ref, dz_ref, sm_ref, cw_ref, al_ref, dtb_ref, gn_ref, cp_ref, s0_ref,
                o_ref, so_ref, xx_ref, st_ref, *, chunk, rows, valid):
    j = pl.program_id(1)

    @pl.when(j == 0)
    def _():
        st_ref[...] = s0_ref[0]
        xx_ref[0:8, :] = cp_ref[0]

    c = chunk
    live = min(c, rows)
    nch = max(rows // c, 1)

    xx_ref[8:8 + rows, 0:GDN_QK] = dq_ref[0]
    xx_ref[8:8 + rows, GDN_QK:2 * GDN_QK] = dk_ref[0]
    xx_ref[8:8 + rows, 2 * GDN_QK:GDN_CONV_CH] = dv_ref[0]
    y = None
    for i in range(CONV_W):
        term = xx_ref[pl.ds(8 - (CONV_W - 1) + i, rows), :] * cw_ref[i:i + 1, :]
        y = term if y is None else y + term
    xx_ref[0:8, :] = xx_ref[rows:rows + 8, :]
    qkv = _silu(y)

    def ext(x):
        if live == c:
            return x
        return jnp.concatenate([x, jnp.zeros((c - live, x.shape[1]), x.dtype)], axis=0)

    ri = _iota((c, c), 0)
    cj = _iota((c, c), 1)
    tri = (ri >= cj).astype(BF16)
    incl = ri >= cj
    strict = ri > cj
    strict_f = strict.astype(F32)
    eye = (ri == cj).astype(F32)
    blk16 = (ri >> 4) == (cj >> 4)
    merge_masks = [((ri >> lv) == (cj >> lv)) & ((ri >> (lv - 1)) != (cj >> (lv - 1)))
                   for lv in range(5, int(math.log2(c)) + 1)]
    pick = (_iota((128, 2 * GDN_VW), 0) == SMALL_DA + (_iota((128, 2 * GDN_VW), 1) >> 7)).astype(BF16)
    lane = _iota((c, 128), 1)

    items = []
    for ci in range(nch):
        sl = pl.ds(ci * c, live)
        blk = ext(qkv[ci * c:ci * c + live])
        dz = ext(dz_ref[0, sl, :])
        sm = ext(sm_ref[0, sl, :])
        g_all = -jnp.exp(al_ref[...]) * _softplus(sm + dtb_ref[...])
        beta_all = jax.nn.sigmoid(sm)
        vals = jnp.where((lane >= SMALL_DA) & (lane < SMALL_DB), g_all,
                         jnp.where((lane >= SMALL_DB) & (lane < SMALL_DB + H_GDN), beta_all, 0.0))
        if valid < c:
            vals = jnp.where(_iota((c, 128), 0) < valid, vals, 0.0)
        gb = _sel_dot_r(vals, pick)
        g_b = gb[:, :GDN_VW]
        gam = _sel_dot_l(tri, g_b)
        egam = jnp.exp(gam)

        for h in range(H_GDN):
            hs = slice(h * DK_GDN, (h + 1) * DK_GDN)
            cq = blk[:, hs]
            ck = blk[:, GDN_QK + h * DK_GDN:GDN_QK + (h + 1) * DK_GDN]
            cv = blk[:, 2 * GDN_QK + h * DV:2 * GDN_QK + (h + 1) * DV]
            qn = cq * lax.rsqrt(jnp.sum(cq * cq, axis=-1, keepdims=True) + RMS_EPS) * (DK_GDN ** -0.5)
            kn = ck * lax.rsqrt(jnp.sum(ck * ck, axis=-1, keepdims=True) + RMS_EPS)
            beta = gb[:, GDN_VW + h * 128:GDN_VW + (h + 1) * 128]
            gm = gam[:, hs]
            eg = egam[:, hs]
            items.append(dict(ci=ci, h=h, sl=sl, qn=qn, kn=kn, beta=beta, gh=g_b[:, hs], gm=gm, eg=eg,
                              gl=gm[c - 1:c, :],
                              rhs=jnp.concatenate([cv * beta, kn * (beta * eg)], axis=1),
                              dz=dz[:, h * DV:(h + 1) * DV]))

    for it in items:
        it["dec"] = jnp.where(incl, jnp.exp(_sel_dot_l(tri, it["gh"] * strict_f)), 0.0)
        it["kk"] = _dot3(it["kn"], it["kn"], (((1,), (1,)), ((), ())))
    for it in items:
        it["nm"] = jnp.where(strict, it["kk"] * it["dec"] * it["beta"], 0.0)
        it["p"] = jnp.where(blk16, it["nm"], 0.0)
        it["t"] = eye - it["p"]
    for _ in range(3):
        for it in items:
            it["p"] = _dot3(it["p"], it["p"])
        for it in items:
            it["t"] = it["t"] + _dot3(it["t"], it["p"])
    for off in merge_masks:
        for it in items:
            it["tc"] = _dot3(it["t"], jnp.where(off, it["nm"], 0.0))
        for it in items:
            it["t"] = it["t"] - _dot3(it["tc"], it["t"])
    for it in items:
        it["x"] = _dot3(it["t"], it["rhs"])
        it["qk"] = jnp.where(incl, _bdot_nt(it["qn"], it["kn"]) * it["dec"], 0.0)

    state = [st_ref[h] for h in range(H_GDN)]
    for ci in range(nch):
        outs = []
        for it in items[ci * H_GDN:(ci + 1) * H_GDN]:
            h, s = it["h"], state[it["h"]]
            u = it["x"][:, :DV] - _bdot(it["x"][:, DV:], s)
            o_h = _bdot(it["qn"] * it["eg"], s) + _bdot(it["qk"], u)
            state[h] = s * jnp.exp(it["gl"]) + _bdot_tn(it["kn"] * jnp.exp(it["gl"] - it["gm"]), u)
            o_h = o_h * lax.rsqrt(jnp.mean(o_h * o_h, axis=-1, keepdims=True) + RMS_EPS) * gn_ref[...]
            outs.append(o_h * _silu(it["dz"]))
        o_full = jnp.concatenate(outs, axis=1)
        o_ref[0, items[ci * H_GDN]["sl"], :] = o_full[:live]
    for h in range(H_GDN):
        st_ref[h] = state[h]

    @pl.when(j == pl.num_programs(1) - 1)
    def _():
        so_ref[0] = st_ref[...]


def gdn_heads(y3, s0, conv_prev8, conv_w, alog_row, dtb_row, gn, *, rows, valid, chunk=128):
    b, l, _ = y3.shape
    blk = lambda idx: pl.BlockSpec((1, rows, 512), lambda bi, j: (bi, j, idx))
    const = lambda shp: pl.BlockSpec(shp, lambda bi, j: tuple(0 for _ in shp))
    return pl.pallas_call(
        functools.partial(_gdn_kernel, chunk=chunk, rows=rows, valid=valid),
        grid=(b, l // rows),
        in_specs=[blk(3), blk(4), blk(5), blk(6),
                  pl.BlockSpec((1, rows, 128), lambda bi, j: (bi, j, LIN_MAIN // 128)),
                  const((CONV_W, GDN_CONV_CH)), const((1, 128)), const((1, 128)), const((1, DV)),
                  pl.BlockSpec((1, 8, GDN_CONV_CH), lambda bi, j: (bi, 0, 0)),
                  pl.BlockSpec((1, H_GDN, DK_GDN, DV), lambda bi, j: (bi, 0, 0, 0))],
        out_specs=[pl.BlockSpec((1, rows, GDN_VW), lambda bi, j: (bi, j, 0)),
                   pl.BlockSpec((1, H_GDN, DK_GDN, DV), lambda bi, j: (bi, 0, 0, 0))],
        out_shape=[jax.ShapeDtypeStruct((b, l, GDN_VW), F32),
                   jax.ShapeDtypeStruct((b, H_GDN, DK_GDN, DV), F32)],
        scratch_shapes=[pltpu.VMEM((rows + 8, GDN_CONV_CH), F32),
                        pltpu.VMEM((H_GDN, DK_GDN, DV), F32)],
        compiler_params=_cparams(("parallel", "arbitrary")),
        name="gdn_heads",
    )(y3, y3, y3, y3, y3, conv_w, alog_row, dtb_row, gn.reshape(1, DV), conv_prev8, s0)


def _tri_ones():
    r = _iota((128, 256), 0)
    c = _iota((128, 256), 1)
    return ((r >= c) | (c >= 128)).astype(BF16)


def _sbp_kernel(q_ref, k_ref, v_ref, b_ref, o_ref, *, tq):
    hp = pl.program_id(1)
    i = pl.program_id(2)
    q = q_ref[0] * (DH_SB ** -0.5 * LOG2E)
    lane = _iota((1, 128), 1)
    to = _tri_ones()
    qm = [jnp.where((lane >= DH_SB) == (hh == 1), q, 0.0).astype(BF16) for hh in range(2)]
    bias = [b_ref[pl.ds(2 * hp + hh, 1), :] * LOG2E for hh in range(2)]
    diag_mask = _iota((tq, tq), 1) < _iota((tq, tq), 0)

    def rows_of(n):
        return pl.ds(pl.multiple_of(jnp.clip(i - n, 0, i) * tq, tq), tq)

    def logits(n, mask):
        kb = k_ref[0, rows_of(n), :].astype(BF16)
        off = jnp.where(n > i, MASKED_LOGIT, 0.0)
        zs = []
        for hh in range(2):
            z = lax.dot_general(qm[hh], kb, (((1,), (1,)), ((), ())), preferred_element_type=F32) + (bias[hh] + off)
            if mask is not None:
                z = jnp.where(mask, z, MASKED_LOGIT)
            zs.append(z)
        return tuple(zs)

    def suffix(zs):
        spbs = [(jnp.maximum(z, 0.0) + jnp.log2(1.0 + jnp.exp2(-jnp.abs(z)))).astype(BF16) for z in zs]
        r_hi = [jnp.dot(spb[:, 128:], to, preferred_element_type=F32) for spb in spbs]
        r_lo = [jnp.dot(spb[:, :128], to, preferred_element_type=F32) for spb in spbs]
        outs = []
        for z, hi, lo in zip(zs, r_hi, r_lo):
            totl = jnp.concatenate([lo[:, :128] + hi[:, 128:], hi[:, :128]], axis=1)
            outs.append((z - totl, lo[:, 128:] + hi[:, 128:]))
        return tuple(outs)

    def weigh(n, ds, state):
        carries, acc = state
        vb = v_ref[0, rows_of(n), :].astype(BF16)
        vcat = jnp.concatenate([jnp.where(lane < DH_SB, vb, 0), jnp.where(lane < DH_SB, 0, vb)], axis=0)
        a = [jnp.exp2(d - jnp.concatenate([c, c], axis=1)).astype(BF16) for (d, _), c in zip(ds, carries)]
        acc = acc + jnp.dot(jnp.concatenate(a, axis=1), vcat, preferred_element_type=F32)
        return tuple(c + rs for (_, rs), c in zip(ds, carries)), acc

    zero = jnp.where(lane < 0, q, 0.0)
    z0 = logits(0, diag_mask)
    idle = tuple((jnp.minimum(z, MASKED_LOGIT), zero) for z in z0)

    def step(t, carried):
        zs, ds, st = carried
        return logits(t + 1, None), suffix(zs), weigh(t - 1, ds, st)

    def body(tu, carried):
        for u in range(SB_UNROLL):
            carried = step(SB_UNROLL * tu + u, carried)
        return carried

    trips = lax.shift_right_logical(i + 2 + SB_UNROLL - 1, SB_UNROLL.bit_length() - 1)
    _, _, state = lax.fori_loop(0, trips, body, (z0, idle, ((zero, zero), zero)))
    o_ref[0] = state[1]


def sb_prompt(y3, bias_rows, tq=256):
    b, l, _ = y3.shape
    nhp = SB_W // 128
    return pl.pallas_call(
        functools.partial(_sbp_kernel, tq=tq),
        grid=(b, nhp, l // tq),
        in_specs=[pl.BlockSpec((1, tq, 128), lambda bi, hp, i: (bi, i, hp)),
                  pl.BlockSpec((1, l, 128), lambda bi, hp, i: (bi, 0, nhp + hp)),
                  pl.BlockSpec((1, l, 128), lambda bi, hp, i: (bi, 0, 2 * nhp + hp)),
                  pl.BlockSpec((H_SB, tq), lambda bi, hp, i: (0, 0))],
        out_specs=pl.BlockSpec((1, tq, 128), lambda bi, hp, i: (bi, i, hp)),
        out_shape=jax.ShapeDtypeStruct((b, l, SB_W), F32),
        compiler_params=_cparams(("parallel", "parallel", "arbitrary")),
        name="sb_prompt",
    )(y3, y3, y3, bias_rows)


def _sbs_kernel(pt_ref, q_ref, kn_ref, vn_ref, *rest, n_tok, group):
    k_refs = rest[:group]
    v_refs = rest[group:2 * group]
    b_ref, o_ref, acc_ref, carry_ref = rest[2 * group:]
    s = pl.program_id(1)
    rows = n_tok * H_SB
    own_head = (_iota((rows, SB_W), 1) >> 6) == (_iota((rows, SB_W), 0) & (H_SB - 1))
    qbd = jnp.where(own_head, q_ref[0] * (DH_SB ** -0.5 * LOG2E), 0.0).astype(BF16)
    causal_new = _iota((rows, PAGE_SIZE), 1) < (_iota((rows, PAGE_SIZE), 0) >> 4)
    to = _tri_ones()
    bias = b_ref[...]

    def sweep(kts, vts, keep):
        zs = [jnp.dot(qbd, kt.astype(BF16), preferred_element_type=F32) + bias for kt in kts]
        if keep is not None:
            zs = [jnp.where(keep, z, MASKED_LOGIT) for z in zs]
        sps = [(jnp.maximum(z, 0.0) + jnp.log2(1.0 + jnp.exp2(-jnp.abs(z)))).astype(BF16) for z in zs]
        rs = [jnp.dot(sp, to, preferred_element_type=F32) for sp in sps]
        carry = carry_ref[...]
        a = []
        for z, r in zip(zs, rs):
            a.append(jnp.exp2(z - (r[:, :128] + carry)).astype(BF16))
            carry = carry + r[:, 128:]
        carry_ref[...] = carry
        a_cat = a[0] if len(a) == 1 else jnp.concatenate(a, axis=1)
        vbs = [vt.astype(BF16) for vt in vts]
        v_cat = vbs[0] if len(vbs) == 1 else jnp.concatenate(vbs, axis=1)
        acc_ref[...] += lax.dot_general(a_cat, v_cat, (((1,), (1,)), ((), ())),
                                        preferred_element_type=F32)

    @pl.when(s == 0)
    def _():
        acc_ref[...] = jnp.zeros_like(acc_ref)
        carry_ref[...] = jnp.zeros_like(carry_ref)
        sweep([kn_ref[0]], [vn_ref[0]], causal_new)

    @pl.when(s > 0)
    def _():
        sweep([k_refs[g][0, 0] for g in range(group)], [v_refs[g][0, 0] for g in range(group)], None)

    @pl.when(s == pl.num_programs(1) - 1)
    def _():
        accm = jnp.where(own_head, acc_ref[...], 0.0)
        o_ref[0] = jnp.concatenate(
            [jnp.sum(accm[t * H_SB:(t + 1) * H_SB], axis=0, keepdims=True) for t in range(n_tok)], axis=0)


def sb_sample(q_s, k_new, v_new, cache_k, cache_v, page_table, bias, layer, group=8):
    b, n_tok, _, _ = q_s.shape
    nl, n_phys = cache_k.shape[:2]
    npages = page_table.shape[1]
    assert H_SB == 16 and npages % group == 0 and n_tok * H_SB <= 128
    rows = n_tok * H_SB
    ck = cache_k.transpose(0, 1, 3, 4, 2).reshape(nl, n_phys, SB_W, PAGE_SIZE)
    cv = cache_v.transpose(0, 1, 3, 4, 2).reshape(nl, n_phys, SB_W, PAGE_SIZE)
    pad = ((0, 0), (0, 0), (0, PAGE_SIZE - n_tok))
    kn = jnp.pad(k_new.reshape(b, n_tok, SB_W).transpose(0, 2, 1), pad)
    vn = jnp.pad(v_new.reshape(b, n_tok, SB_W).transpose(0, 2, 1), pad)
    qrep = jnp.repeat(q_s.reshape(b, n_tok, SB_W), H_SB, axis=1)
    bias_rows = jnp.broadcast_to((jnp.tile(bias, n_tok) * LOG2E)[:, None], (rows, PAGE_SIZE))

    def page_map(k):
        return lambda bi, s, pt: (layer, pt[bi, npages - 1 - (jnp.maximum(s, 1) - 1) * group - k], 0, 0)

    page_specs = [pl.BlockSpec((1, 1, SB_W, PAGE_SIZE), page_map(k)) for k in range(group)]
    grid_spec = pltpu.PrefetchScalarGridSpec(
        num_scalar_prefetch=1,
        grid=(b, npages // group + 1),
        in_specs=[pl.BlockSpec((1, rows, SB_W), lambda bi, s, pt: (bi, 0, 0)),
                  pl.BlockSpec((1, SB_W, PAGE_SIZE), lambda bi, s, pt: (bi, 0, 0)),
                  pl.BlockSpec((1, SB_W, PAGE_SIZE), lambda bi, s, pt: (bi, 0, 0))]
                 + page_specs + page_specs
                 + [pl.BlockSpec((rows, PAGE_SIZE), lambda bi, s, pt: (0, 0))],
        out_specs=pl.BlockSpec((1, n_tok, SB_W), lambda bi, s, pt: (bi, 0, 0)),
        scratch_shapes=[pltpu.VMEM((rows, SB_W), F32), pltpu.VMEM((rows, PAGE_SIZE), F32)],
    )
    out = pl.pallas_call(
        functools.partial(_sbs_kernel, n_tok=n_tok, group=group),
        grid_spec=grid_spec,
        out_shape=jax.ShapeDtypeStruct((b, n_tok, SB_W), F32),
        compiler_params=_cparams(("parallel", "arbitrary")),
        name="sb_sample",
    )(page_table, qrep, kn, vn, *([ck] * group), *([cv] * group), bias_rows)
    return out.reshape(b, n_tok, H_SB, DH_SB)


def _router_kernel(x_ref, sc_ref, sh_ref, wrt_ref, eb_ref, o_ref):
    h = x_ref[...] * (1.0 + sc_ref[0]) + sh_ref[0]
    tm = h.shape[0]
    hh, hl = _hilo(h)
    wh, wl = _hilo(wrt_ref[...])
    nt = lambda a, b: lax.dot_general(a, b, (((1,), (1,)), ((), ())), preferred_element_type=F32)
    logits = nt(wh, hh) + nt(wh, hl) + nt(wl, hh)
    s = jax.nn.sigmoid(logits)
    sel = s + eb_ref[...]
    gsz = N_EXPERTS // N_GROUPS
    ninf = -jnp.inf
    g3 = sel.reshape(N_GROUPS, gsz, tm)
    io3 = _iota((N_GROUPS, gsz, tm), 1).astype(F32)
    m1 = jnp.max(g3, axis=1, keepdims=True)
    i1 = jnp.min(jnp.where(g3 == m1, io3, float(gsz)), axis=1, keepdims=True)
    m2 = jnp.max(jnp.where(io3 == i1, ninf, g3), axis=1, keepdims=True)
    gscore = (m1 + m2).reshape(N_GROUPS, tm)
    iog = _iota((N_GROUPS, tm), 0).astype(F32)
    gsel = jnp.zeros((N_GROUPS, tm), F32)
    cur = gscore
    for _ in range(TOPK_GROUPS):
        m = jnp.max(cur, axis=0, keepdims=True)
        idx = jnp.min(jnp.where(cur == m, iog, float(N_GROUPS)), axis=0, keepdims=True)
        hit = iog == idx
        gsel = jnp.where(hit, 1.0, gsel)
        cur = jnp.where(hit, ninf, cur)
    emask = jnp.broadcast_to(gsel.reshape(N_GROUPS, 1, tm), (N_GROUPS, gsz, tm)).reshape(N_EXPERTS, tm)
    cur = jnp.where(emask > 0.0, sel, ninf)
    ioe = _iota((N_EXPERTS, tm), 0).astype(F32)
    chosen = jnp.zeros((N_EXPERTS, tm), F32)
    for _ in range(TOP_K):
        m = jnp.max(cur, axis=0, keepdims=True)
        idx = jnp.min(jnp.where(cur == m, ioe, float(N_EXPERTS)), axis=0, keepdims=True)
        hit = ioe == idx
        chosen = jnp.where(hit, 1.0, chosen)
        cur = jnp.where(hit, ninf, cur)
    wsel = jnp.where(chosen > 0.0, s, 0.0)
    o_ref[...] = wsel / jnp.sum(wsel, axis=0, keepdims=True) * ROUTED_SCALE


def router(x, sc, sh, w_r_t, e_b, tm):
    t, d = x.shape
    nb, r, _ = sc.shape
    tpb = (t // tm) // nb
    mod_spec = pl.BlockSpec((1, r, d), lambda i: (i // tpb, 0, 0))
    return pl.pallas_call(
        _router_kernel,
        grid=(t // tm,),
        in_specs=[pl.BlockSpec((tm, d), lambda i: (i, 0)), mod_spec, mod_spec,
                  pl.BlockSpec((N_EXPERTS, d), lambda i: (0, 0)),
                  pl.BlockSpec((N_EXPERTS, 1), lambda i: (0, 0))],
        out_specs=pl.BlockSpec((N_EXPERTS, tm), lambda i: (0, i)),
        out_shape=jax.ShapeDtypeStruct((N_EXPERTS, t), F32),
        compiler_params=_cparams(("parallel",)),
        name="moe_router",
    )(x, sc, sh, w_r_t, e_b.reshape(N_EXPERTS, 1))


def _moe_kernel(x_ref, sc_ref, sh_ref, ga_ref, gates_ref, wg_ref, wu_ref, wd_ref, sg_ref, su_ref, sd_ref,
                lg_ref, lb_ref, o_ref, hb_ref, acc_ref, g2_ref, *, eps):
    e = pl.program_id(1)
    n_routed_steps = N_EXPERTS // eps

    @pl.when(e == 0)
    def _():
        hb_ref[...] = (x_ref[...] * (1.0 + sc_ref[0]) + sh_ref[0]).astype(BF16)
        acc_ref[...] = jnp.zeros_like(acc_ref)
        g = gates_ref[...]
        gh = g.astype(BF16)
        g2_ref[:, :N_EXPERTS] = gh
        g2_ref[:, N_EXPERTS:] = (g - gh.astype(F32)).astype(BF16)

    def experts(wgs, wus, wd_rows, gcols):
        f = wgs[0].shape[1]
        w_in = [w.astype(BF16) for pair in zip(wgs, wus) for w in pair]
        gu = jnp.dot(hb_ref[...], w_in[0] if len(w_in) == 1 else jnp.concatenate(w_in, axis=1),
                     preferred_element_type=F32)
        acts = []
        for k, gcol in enumerate(gcols):
            a = _silu(gu[:, 2 * k * f:(2 * k + 1) * f]) * gu[:, (2 * k + 1) * f:(2 * k + 2) * f]
            acts.append((a if gcol is None else a * gcol).astype(BF16))
        a_cat = acts[0] if len(acts) == 1 else jnp.concatenate(acts, axis=1)
        acc_ref[...] += jnp.dot(a_cat, wd_rows.astype(BF16), preferred_element_type=F32)

    @pl.when(e < n_routed_steps)
    def _():
        gcols = []
        for k in range(eps):
            pick = ((_iota((2 * N_EXPERTS, D_EXPERT), 0) & (N_EXPERTS - 1)) == e * eps + k).astype(BF16)
            gcols.append(jnp.dot(g2_ref[...], pick, preferred_element_type=F32))
        wd = wd_ref[...]
        experts([wg_ref[k] for k in range(eps)], [wu_ref[k] for k in range(eps)],
                wd.reshape(wd.shape[0] * wd.shape[1], wd.shape[2]), gcols)

    @pl.when(e == n_routed_steps)
    def _():
        experts([sg_ref[...]], [su_ref[...]], sd_ref[...], [None])
        r = DEEPNORM_ALPHA * x_ref[...] + (1.0 + ga_ref[0]) * acc_ref[...]
        o_ref[...] = _layer_norm(r, lg_ref[...], lb_ref[...])


def moe_ln(x, sc, sh, ga, gates, wg, wu, wd, sg, su, sd, ln_g, ln_b, layer, tm, eps=2):
    t, d = x.shape
    nb, r, _ = sc.shape
    tpb = (t // tm) // nb
    f = wg.shape[3]
    last = N_EXPERTS // eps - 1
    mod_spec = pl.BlockSpec((1, r, d), lambda i, e: (i // tpb, 0, 0))
    const2 = lambda shp: pl.BlockSpec(shp, lambda i, e: (0, 0))
    routed = lambda a, b: pl.BlockSpec((None, eps, a, b), lambda i, e: (layer, jnp.minimum(e, last), 0, 0))
    shared = lambda a, b: pl.BlockSpec((None, a, b), lambda i, e: (layer, 0, 0))
    return pl.pallas_call(
        functools.partial(_moe_kernel, eps=eps),
        grid=(t // tm, N_EXPERTS // eps + 1),
        in_specs=[pl.BlockSpec((tm, d), lambda i, e: (i, 0)), mod_spec, mod_spec, mod_spec,
                  pl.BlockSpec((tm, N_EXPERTS), lambda i, e: (i, 0)),
                  routed(d, f), routed(d, f), routed(f, d),
                  shared(d, f), shared(d, f), shared(f, d), const2((1, d)), const2((1, d))],
        out_specs=pl.BlockSpec((tm, d), lambda i, e: (i, 0)),
        out_shape=jax.ShapeDtypeStruct((t, d), F32),
        scratch_shapes=[pltpu.VMEM((tm, d), BF16), pltpu.VMEM((tm, d), F32),
                        pltpu.VMEM((tm, 2 * N_EXPERTS), BF16)],
        compiler_params=_cparams(("parallel", "arbitrary")),
        name="moe_experts_ln",
    )(x, sc, sh, ga, gates, wg, wu, wd, sg, su, sd, ln_g.reshape(1, d), ln_b.reshape(1, d))


def _split_mods(m, per_row_repeat):
    parts = jnp.split(m, 6, axis=-1)
    if per_row_repeat is None:
        return [p[:, None, :] for p in parts]
    return [jnp.repeat(p, per_row_repeat, axis=0)[None] for p in parts]


def kernel(x_prompt, x_sample, state_gla, state_gdn, state_conv, cache_k, cache_v, page_table, c_prompt, c_sample, w_ada, b_ada, ln_g, ln_b, w_in_lin, gla_wa2, gla_ba, gla_norm, gdn_conv, gdn_a_log, gdn_dt_bias, gdn_norm, w_out_lin, w_qkv_sb, w_o_sb, sb_bias, w_router, e_bias, w_gate, w_up, w_down, ws_gate, ws_up, ws_down):
    bp, seq, d = x_prompt.shape
    bs, dseq, _ = x_sample.shape
    tp = bp * seq
    ts = bs * dseq
    n_phys = cache_k.shape[1]

    xp = x_prompt.reshape(tp, d)
    xs = x_sample.reshape(ts, d)

    mods = ada_all(jnp.concatenate([c_prompt, c_sample], axis=0), w_ada, b_ada)

    w_lin = jnp.concatenate(
        [w_in_lin[:, :, :1536], w_in_lin[:, :, 1552:3600], w_in_lin[:, :, 1536:1552], w_in_lin[:, :, 3600:3608],
         jnp.zeros((w_in_lin.shape[0], d, LIN_COLS - LIN_MAIN - GLA_RANK - 2 * H_GDN), w_in_lin.dtype)],
        axis=-1).astype(BF16)
    w_out = w_out_lin.astype(BF16)
    w_qkv = w_qkv_sb.astype(BF16)
    w_o = w_o_sb.astype(BF16)
    w_r_t = jnp.swapaxes(w_router, 1, 2)
    tail = jnp.zeros((w_in_lin.shape[0], 1, 128), F32)
    alog_row = tail.at[:, 0, SMALL_DA:SMALL_DB].set(gdn_a_log)
    dtb_row = tail.at[:, 0, SMALL_DA:SMALL_DB].set(gdn_dt_bias)

    pad_rows = 8
    gla_p, gla_s, gdn_p, gdn_s, conv_p, conv_s = [], [], [], [], [], []
    k_p, v_p, k_s, v_s = [], [], [], []
    for layer in range(DEPTH):
        i = layer // 2
        sh_p, sc_p, ga_p, shf_p, scf_p, gaf_p = _split_mods(mods[layer, :bp], None)
        sh_s, sc_s, ga_s, shf_s, scf_s, gaf_s = _split_mods(mods[layer, bp:bp + bs], dseq)
        if layer % 2 == 0:
            yp = mod_matmul(xp, sc_p, sh_p, w_lin[i], tm=1024, tn=LIN_COLS // 3)
            ys = mod_matmul(xs, sc_s, sh_s, w_lin[i], tm=ts, tn=LIN_COLS // 5)
            yp3 = yp.reshape(bp, seq, LIN_COLS)
            ys3 = jnp.pad(ys.reshape(bs, dseq, LIN_COLS), ((0, 0), (0, pad_rows - dseq), (0, 0)))
            zeros_t = jnp.zeros((bp, DV, GLA_QK), F32)
            s0_t = state_gla[i].transpose(0, 3, 1, 2).reshape(bs, DV, GLA_QK)
            og_p, st_p = gla_heads(yp3, zeros_t, gla_wa2[i], gla_ba[i], gla_norm[i], rows=256, valid=256)
            og_s, st_s = gla_heads(ys3, s0_t, gla_wa2[i], gla_ba[i], gla_norm[i], rows=pad_rows, valid=dseq)
            gla_p.append(st_p.reshape(bp, DV, H_GLA, DK_GLA).transpose(0, 2, 3, 1))
            gla_s.append(st_s.reshape(bs, DV, H_GLA, DK_GLA).transpose(0, 2, 3, 1))
            cp_p = jnp.zeros((bp, 8, GDN_CONV_CH), F32)
            cp_s = jnp.pad(state_conv[i], ((0, 0), (8 - (CONV_W - 1), 0), (0, 0)))
            od_p, sd_p = gdn_heads(yp3, jnp.zeros((bp, H_GDN, DK_GDN, DV), F32), cp_p, gdn_conv[i],
                                   alog_row[i], dtb_row[i], gdn_norm[i], rows=256, valid=256)
            od_s, sd_s = gdn_heads(ys3, state_gdn[i], cp_s, gdn_conv[i],
                                   alog_row[i], dtb_row[i], gdn_norm[i], rows=pad_rows, valid=dseq)
            gdn_p.append(sd_p)
            gdn_s.append(sd_s)
            c0, c1 = 2 * GLA_QK + 2 * GLA_VW, 2 * GLA_QK + 2 * GLA_VW + GDN_CONV_CH
            conv_p.append(yp3[:, seq - (CONV_W - 1):, c0:c1])
            conv_s.append(jnp.concatenate([state_conv[i], ys.reshape(bs, dseq, LIN_COLS)[:, :, c0:c1]],
                                          axis=1)[:, dseq:])
            xp = proj_ln([og_p.reshape(tp, GLA_VW), od_p.reshape(tp, GDN_VW)],
                         [w_out[i, :GLA_VW], w_out[i, GLA_VW:]], xp, ga_p, ln_g[layer, 0], ln_b[layer, 0], tm=1024)
            xs = proj_ln([og_s[:, :dseq].reshape(ts, GLA_VW), od_s[:, :dseq].reshape(ts, GDN_VW)],
                         [w_out[i, :GLA_VW], w_out[i, GLA_VW:]], xs, ga_s, ln_g[layer, 0], ln_b[layer, 0], tm=ts)
        else:
            yp = mod_matmul(xp, sc_p, sh_p, w_qkv[i], tm=1024, tn=1536)
            ys = mod_matmul(xs, sc_s, sh_s, w_qkv[i], tm=ts, tn=1024)
            yp3 = yp.reshape(bp, seq, 3 * SB_W)
            ys3 = ys.reshape(bs, dseq, 3 * SB_W)
            bias = sb_bias[i]
            o_p = sb_prompt(yp3, jnp.broadcast_to(bias[:, None], (H_SB, 256)))
            q_s, kn_s, vn_s = (ys3[:, :, j * SB_W:(j + 1) * SB_W].reshape(bs, dseq, H_SB, DH_SB) for j in range(3))
            o_s = sb_sample(q_s, kn_s, vn_s, cache_k, cache_v, page_table, bias, i)
            k_p.append(yp3[:, :, SB_W:2 * SB_W].reshape(bp, seq, H_SB, DH_SB))
            v_p.append(yp3[:, :, 2 * SB_W:].reshape(bp, seq, H_SB, DH_SB))
            k_s.append(kn_s)
            v_s.append(vn_s)
            xp = proj_ln([o_p.reshape(tp, SB_W)], [w_o[i]], xp, ga_p, ln_g[layer, 0], ln_b[layer, 0], tm=1024)
            xs = proj_ln([o_s.reshape(ts, SB_W)], [w_o[i]], xs, ga_s, ln_g[layer, 0], ln_b[layer, 0], tm=ts)
        mw = (w_gate, w_up, w_down, ws_gate, ws_up, ws_down, ln_g[layer, 1], ln_b[layer, 1], layer)
        gates_p = router(xp, scf_p, shf_p, w_r_t[layer], e_bias[layer], tm=512).T
        gates_s = router(xs, scf_s, shf_s, w_r_t[layer], e_bias[layer], tm=ts).T
        xp = moe_ln(xp, scf_p, shf_p, gaf_p, gates_p, *mw, tm=1024)
        xs = moe_ln(xs, scf_s, shf_s, gaf_s, gates_s, *mw, tm=ts)
    return (xp.reshape(bp, seq, d), xs.reshape(bs, dseq, d), jnp.stack(gla_p), jnp.stack(gla_s),
            jnp.stack(gdn_p), jnp.stack(gdn_s), jnp.stack(conv_p), jnp.stack(conv_s),
            jnp.stack(k_p), jnp.stack(v_p), jnp.stack(k_s), jnp.stack(v_s))
```
